```python
import jax, jax.numpy as jnp
from jax import lax
import numpy as np

D_MODEL = 1024
BATCH = 2
SEQ = 8192
DEPTH = 1

GRID_W = 64
CTX_LEN = 256
M_HEADS = 4
M_DH = 128
M_W = M_HEADS * M_DH
M_CHUNK = 128
CONV_W = 3
N_GATES = 4 * M_HEADS
A_HEADS = 8
A_KV = 2
A_REP = A_HEADS // A_KV
A_DH = 64
A_W = A_HEADS * A_DH
A_KVW = A_KV * A_DH
Q_BLOCK = 128
ROPE_THETA = 10000.0
ATTN_SCALE = A_DH ** -0.5
MIX_W = M_W + A_W
G_END = 4 * M_W + N_GATES
PROJ_W = G_END + A_W + 2 * A_KVW
P_HEADS = 8
N_KEYS = 128
N_EXPERTS = N_KEYS * N_KEYS
P_QDIM = 256
P_HALF = P_QDIM // 2
P_TOPK = 16
P_BLOCK = 128
N_MOD = 6
DEEPNORM_ALPHA = (2.0 * DEPTH) ** 0.25
DEEPNORM_BETA = (8.0 * DEPTH) ** -0.25
LN_EPS = 1e-5
RMS_EPS = 1e-6

kernel_name = 'hybrid_mlstm_gqa_peer_dit_block'


def _layer_norm(x, g, b):
    xf = x.astype(jnp.float32)
    mu = jnp.mean(xf, axis=-1, keepdims=True)
    var = jnp.mean(jnp.square(xf - mu), axis=-1, keepdims=True)
    y = (xf - mu) * lax.rsqrt(var + LN_EPS) * g.astype(jnp.float32) + b.astype(jnp.float32)
    return y.astype(x.dtype)


def _rms_norm(x, w):
    xf = x.astype(jnp.float32)
    y = xf * lax.rsqrt(jnp.mean(jnp.square(xf), axis=-1, keepdims=True) + RMS_EPS) * w.astype(jnp.float32)
    return y.astype(x.dtype)


def _dwconv_centred(x, w, b):
    y = lax.conv_general_dilated(
        x, w[:, None, :].astype(x.dtype), window_strides=(1,),
        padding=[((CONV_W - 1) // 2, CONV_W // 2)],
        dimension_numbers=('NWC', 'WIO', 'NWC'), feature_group_count=x.shape[-1])
    return y + b.astype(x.dtype)


def _rope_axis(x, pos):
    half = x.shape[-1] // 2
    freq = ROPE_THETA ** (-jnp.arange(half, dtype=jnp.float32) / half)
    ang = pos[:, None] * freq[None, :]
    cos, sin = jnp.cos(ang), jnp.sin(ang)
    x1, x2 = x[..., :half], x[..., half:]
    return jnp.concatenate([x1 * cos - x2 * sin, x1 * sin + x2 * cos], axis=-1)


def _rope_2d(x, row, col):
    h = x.shape[-1] // 2
    y = jnp.concatenate([_rope_axis(x[..., :h], row), _rope_axis(x[..., h:], col)], axis=-1)
    return y.astype(x.dtype)


def _project(h, w_in, b_gates, conv_w, conv_b, qn_w, kn_w):
    B_, L_, _ = h.shape
    z = h @ w_in
    qk_m = jax.nn.silu(_dwconv_centred(z[..., :2 * M_W], conv_w, conv_b))

    def heads(t):
        return t.reshape(B_, L_, M_HEADS, M_DH).transpose(0, 2, 1, 3).astype(jnp.float32)

    q_m = heads(qk_m[..., :M_W]) * (M_DH ** -0.5)
    k_m = heads(qk_m[..., M_W:])
    v_m = heads(z[..., 2 * M_W:3 * M_W])
    o_m = z[..., 3 * M_W:4 * M_W]
    gates = (z[..., 4 * M_W:G_END].astype(jnp.float32) + b_gates.astype(jnp.float32))
    gates = gates.reshape(B_, L_, 4, M_HEADS).transpose(2, 0, 3, 1)
    a0 = G_END
    q_a = _rms_norm(z[..., a0:a0 + A_W].reshape(B_, L_, A_HEADS, A_DH), qn_w)
    q_a = q_a.transpose(0, 2, 1, 3).reshape(B_, A_KV, A_REP, L_, A_DH)
    k_a = _rms_norm(z[..., a0 + A_W:a0 + A_W + A_KVW].reshape(B_, L_, A_KV, A_DH), kn_w).transpose(0, 2, 1, 3)
    v_a = z[..., a0 + A_W + A_KVW:].reshape(B_, L_, A_KV, A_DH).transpose(0, 2, 1, 3)
    return q_m, k_m, v_m, o_m, gates, q_a, k_a, v_a


def _zero_state(b):
    return (jnp.zeros((b, M_HEADS, M_DH, M_DH), jnp.float32),
            jnp.zeros((b, M_HEADS, M_DH), jnp.float32),
            jnp.zeros((b, M_HEADS), jnp.float32))


def _mlstm_scan(q, k, v, li, lf, state):
    B_, H_, L_, DH_ = q.shape
    nc = L_ // M_CHUNK

    def chunks(t):
        return jnp.moveaxis(t.reshape((B_, H_, nc, M_CHUNK) + t.shape[3:]), 2, 0)

    mask = jnp.tril(jnp.ones((M_CHUNK, M_CHUNK), dtype=bool))

    def step(carry, inp):
        C, n, m = carry
        qc, kc, vc, lic, lfc = inp
        b = jnp.cumsum(lfc, axis=-1)
        logd = jnp.where(mask, b[..., :, None] - b[..., None, :] + lic[..., None, :], -jnp.inf)
        inter = b + m[..., None]
        m_s = jnp.maximum(inter, jnp.max(logd, axis=-1))
        dmat = jnp.exp(logd - m_s[..., None])
        s = jnp.einsum('bhsd,bhrd->bhsr', qc, kc) * dmat
        w_inter = jnp.exp(inter - m_s)
        num = jnp.einsum('bhsr,bhrd->bhsd', s, vc) + w_inter[..., None] * jnp.einsum('bhvk,bhsk->bhsv', C, qc)
        den = jnp.sum(s, axis=-1) + w_inter * jnp.einsum('bhk,bhsk->bhs', n, qc)
        h = num / jnp.maximum(jnp.abs(den), jnp.exp(-m_s))[..., None]
        log_w = b[..., -1:] - b + lic
        m_new = jnp.maximum(b[..., -1] + m, jnp.max(log_w, axis=-1))
        w = jnp.exp(log_w - m_new[..., None])
        decay = jnp.exp(b[..., -1] + m - m_new)
        C_new = decay[..., None, None] * C + jnp.einsum('bhr,bhrv,bhrk->bhvk', w, vc, kc)
        n_new = decay[..., None] * n + jnp.einsum('bhr,bhrk->bhk', w, kc)
        return (C_new, n_new, m_new), h

    state, hs = lax.scan(step, state, (chunks(q), chunks(k), chunks(v), chunks(li), chunks(lf)))
    h = jnp.moveaxis(hs, 0, 2).reshape(B_, H_, L_, DH_)
    return h, state


def _mlstm_bidir(q, k, v, gates, state_f, state_b):
    i_f, f_f, i_b, f_b = gates[0], gates[1], gates[2], gates[3]
    h_f, st_f = _mlstm_scan(q, k, v, i_f, jax.nn.log_sigmoid(f_f), state_f)

    def fl(t):
        return jnp.flip(t, axis=2)

    h_b, st_b = _mlstm_scan(fl(q), fl(k), fl(v), fl(i_b), fl(jax.nn.log_sigmoid(f_b)), state_b)
    return h_f + fl(h_b), st_f, st_b


def _mlstm_out(h, o, w):
    B_, H_, L_, DH_ = h.shape
    mu = jnp.mean(h, axis=-1, keepdims=True)
    var = jnp.mean(jnp.square(h - mu), axis=-1, keepdims=True)
    hn = (h - mu) * lax.rsqrt(var + LN_EPS) * w.astype(jnp.float32).reshape(M_HEADS, 1, M_DH)
    hn = hn.transpose(0, 2, 1, 3).reshape(B_, L_, M_W)
    return (jax.nn.sigmoid(o.astype(jnp.float32)) * hn).astype(o.dtype)


def _attn_latent(q, k_lat, v_lat, k_ctx, v_ctx):
    B_, G_, R_, L_, DH_ = q.shape
    k = jnp.concatenate([k_lat, k_ctx], axis=2)
    v = jnp.concatenate([v_lat, v_ctx], axis=2)
    nb = L_ // Q_BLOCK
    qb = jnp.moveaxis(q.reshape(B_, G_, R_, nb, Q_BLOCK, DH_), 3, 0)

    def block(qi):
        s = jnp.einsum('bgrqd,bgkd->bgrqk', qi, k) * ATTN_SCALE
        p = jax.nn.softmax(s.astype(jnp.float32), axis=-1).astype(v.dtype)
        return jnp.einsum('bgrqk,bgkd->bgrqd', p, v)

    o = lax.map(block, qb)
    return o.transpose(1, 0, 4, 2, 3, 5).reshape(B_, L_, A_W)


def _attn_context(q, k, v):
    B_, G_, R_, L_, DH_ = q.shape
    s = jnp.einsum('bgrqd,bgkd->bgrqk', q, k) * ATTN_SCALE
    p = jax.nn.softmax(s.astype(jnp.float32), axis=-1).astype(v.dtype)
    o = jnp.einsum('bgrqk,bgkd->bgrqd', p, v)
    return o.transpose(0, 3, 1, 2, 4).reshape(B_, L_, A_W)


def _peer(h, wq, keys, u, v):
    B_, L_, D_ = h.shape
    xb = h.reshape(-1, P_BLOCK, D_)

    def block(xt):
        q = (xt @ wq).reshape(P_BLOCK, P_HEADS, 2, P_HALF)
        s = jnp.einsum('nhpd,hpkd->nhpk', q, keys)
        s1, i1 = lax.top_k(s[:, :, 0], P_TOPK)
        s2, i2 = lax.top_k(s[:, :, 1], P_TOPK)
        cand_s = (s1[..., :, None] + s2[..., None, :]).reshape(P_BLOCK, P_HEADS, P_TOPK * P_TOPK)
        cand_i = (i1[..., :, None] * N_KEYS + i2[..., None, :]).reshape(P_BLOCK, P_HEADS, P_TOPK * P_TOPK)
        top_s, pos = lax.top_k(cand_s, P_TOPK)
        idx = jnp.take_along_axis(cand_i, pos, axis=-1).reshape(P_BLOCK, P_HEADS * P_TOPK)
        g = jax.nn.softmax(top_s.astype(jnp.float32), axis=-1).reshape(P_BLOCK, P_HEADS * P_TOPK).astype(xt.dtype)
        act = jax.nn.gelu(jnp.einsum('nd,ned->ne', xt, u[idx]), approximate=False)
        return jnp.einsum('ne,ned->nd', g * act, v[idx])

    return lax.map(block, xb).reshape(B_, L_, D_)


def setup_inputs(seed: int = 0) -> dict:
    key = jax.random.key(seed)
    ks = jax.random.split(key, 24)
    f32 = jnp.float32

    def nrm(k, shape, scale):
        return jax.random.normal(k, shape, f32) * scale

    f_bias = jnp.linspace(3.0, 6.0, M_HEADS, dtype=f32)
    gate_sel = jnp.array([0.0, 1.0, 0.0, 1.0], dtype=f32)
    b_gates = (nrm(ks[9], (DEPTH, 4, M_HEADS), 0.1) + gate_sel[None, :, None] * f_bias[None, None, :]).reshape(DEPTH, N_GATES)
    return {
        'x': nrm(ks[0], (BATCH, SEQ, D_MODEL), 1.0),
        'c': nrm(ks[1], (BATCH, D_MODEL), 1.0),
        'ctx': nrm(ks[2], (BATCH, CTX_LEN, D_MODEL), 1.0),
        'c_ctx': nrm(ks[3], (D_MODEL,), 1.0),
        'w_mod': nrm(ks[4], (DEPTH, D_MODEL, N_MOD * D_MODEL), 0.5 * D_MODEL ** -0.5),
        'b_mod': nrm(ks[5], (DEPTH, N_MOD * D_MODEL), 0.01),
        'w_in': nrm(ks[6], (DEPTH, D_MODEL, PROJ_W), D_MODEL ** -0.5),
        'conv_w': nrm(ks[7], (DEPTH, CONV_W, 2 * M_W), CONV_W ** -0.5),
        'conv_b': nrm(ks[8], (DEPTH, 2 * M_W), 0.01),
        'b_gates': b_gates,
        'mh_norm_w': 1.0 + nrm(ks[10], (DEPTH, M_W), 0.01),
        'q_norm_w': 1.0 + nrm(ks[11], (DEPTH, A_DH), 0.01),
        'k_norm_w': 1.0 + nrm(ks[12], (DEPTH, A_DH), 0.01),
        'w_out': nrm(ks[13], (DEPTH, MIX_W, D_MODEL), DEEPNORM_BETA * MIX_W ** -0.5),
        'ln1_g': 1.0 + nrm(ks[14], (DEPTH, D_MODEL), 0.01),
        'ln1_b': nrm(ks[15], (DEPTH, D_MODEL), 0.01),
        'peer_wq': nrm(ks[16], (DEPTH, D_MODEL, P_HEADS * P_QDIM), D_MODEL ** -0.5),
        'peer_keys': nrm(ks[17], (DEPTH, P_HEADS, 2, N_KEYS, P_HALF), P_HALF ** -0.5),
        'peer_u': nrm(ks[18], (DEPTH, N_EXPERTS, D_MODEL), D_MODEL ** -0.5),
        'peer_v': nrm(ks[19], (DEPTH, N_EXPERTS, D_MODEL), DEEPNORM_BETA),
        'ln2_g': 1.0 + nrm(ks[20], (DEPTH, D_MODEL), 0.01),
        'ln2_b': nrm(ks[21], (DEPTH, D_MODEL), 0.01),
    }


def reference(x, c, ctx, c_ctx, w_mod, b_mod, w_in, conv_w, conv_b, b_gates, mh_norm_w,
              q_norm_w, k_norm_w, w_out, ln1_g, ln1_b, peer_wq, peer_keys, peer_u, peer_v,
              ln2_g, ln2_b):
    n_tok = x.shape[1]
    ROWS = n_tok // GRID_W
    row = jnp.broadcast_to(jnp.arange(ROWS, dtype=jnp.float32)[:, None], (ROWS, GRID_W)).reshape(-1)
    col = jnp.broadcast_to(jnp.arange(GRID_W, dtype=jnp.float32)[None, :], (ROWS, GRID_W)).reshape(-1)
    for l in range(DEPTH):
        last = l == DEPTH - 1
        mod = jax.nn.silu(c) @ w_mod[l] + b_mod[l]
        mod_c = jax.nn.silu(c_ctx) @ w_mod[l] + b_mod[l]
        sh1, sc1, g1, sh2, sc2, g2 = [m[:, None, :] for m in jnp.split(mod, N_MOD, axis=-1)]
        csh1, csc1, cg1, csh2, csc2, cg2 = jnp.split(mod_c, N_MOD, axis=-1)
        proj_args = (w_in[l], b_gates[l], conv_w[l], conv_b[l], q_norm_w[l], k_norm_w[l])

        h_ctx = ctx * (1 + csc1) + csh1
        qm_c, km_c, vm_c, om_c, gt_c, qa_c, ka_c, va_c = _project(h_ctx, *proj_args)
        zero = _zero_state(ctx.shape[0])
        hm_c, st_f, st_b = _mlstm_bidir(qm_c, km_c, vm_c, gt_c, zero, zero)

        h_lat = x * (1 + sc1) + sh1
        qm, km, vm, om, gt, qa, ka, va = _project(h_lat, *proj_args)
        hm, _, _ = _mlstm_bidir(qm, km, vm, gt, st_f, st_b)
        qa = _rope_2d(qa, row, col)
        ka = _rope_2d(ka, row, col)
        att = _attn_latent(qa, ka, va, ka_c, va_c)
        mix = jnp.concatenate([_mlstm_out(hm, om, mh_norm_w[l]), att], axis=-1) @ w_out[l]
        x = _layer_norm(DEEPNORM_ALPHA * x + g1 * mix, ln1_g[l], ln1_b[l])
        y = _peer(x * (1 + sc2) + sh2, peer_wq[l], peer_keys[l], peer_u[l], peer_v[l])
        x = _layer_norm(DEEPNORM_ALPHA * x + g2 * y, ln2_g[l], ln2_b[l])

        if not last:
            att_c = _attn_context(qa_c, ka_c, va_c)
            mix_c = jnp.concatenate([_mlstm_out(hm_c, om_c, mh_norm_w[l]), att_c], axis=-1) @ w_out[l]
            ctx = _layer_norm(DEEPNORM_ALPHA * ctx + cg1 * mix_c, ln1_g[l], ln1_b[l])
            y_c = _peer((ctx * (1 + csc2) + csh2)[None], peer_wq[l], peer_keys[l], peer_u[l], peer_v[l])[0] if ctx.ndim == 2 else _peer(ctx * (1 + csc2) + csh2, peer_wq[l], peer_keys[l], peer_u[l], peer_v[l])
            ctx = _layer_norm(DEEPNORM_ALPHA * ctx + cg2 * y_c, ln2_g[l], ln2_b[l])
    return x
```

```python
import functools

import jax
import jax.numpy as jnp
from jax import lax
from jax.experimental import pallas as pl
from jax.experimental.pallas import tpu as pltpu

F32 = jnp.float32
BF16 = jnp.bfloat16

D_MODEL = 1024
DEPTH = 1
GRID_W = 64
M_HEADS = 4
M_DH = 128
M_W = M_HEADS * M_DH
M_CHUNK = 128
N_GATES = 4 * M_HEADS
A_HEADS = 8
A_KV = 2
A_REP = A_HEADS // A_KV
A_DH = 64
A_W = A_HEADS * A_DH
A_KVW = A_KV * A_DH
ROPE_THETA = 10000.0
ATTN_SCALE = A_DH ** -0.5
G_END = 4 * M_W + N_GATES
P_HEADS = 8
N_KEYS = 128
P_HALF = 128
P_QDIM = 2 * P_HALF
P_TOPK = 16
N_MOD = 6
DEEPNORM_ALPHA = (2.0 * DEPTH) ** 0.25
LN_EPS = 1e-5
RMS_EPS = 1e-6

LANES = 128
SUBLANES = 8
VMEM_LIMIT = 56 * 1024 * 1024

C_QK = 0
C_VM = 2 * M_W
C_OM = 3 * M_W
C_G = 4 * M_W
C_QA = C_G + LANES
C_KA = C_QA + A_HEADS * LANES
C_VA = C_KA + LANES
C_END = C_VA + LANES

NEG_INF = float("-inf")


def _cparams(sem):
    return pltpu.CompilerParams(dimension_semantics=sem, vmem_limit_bytes=VMEM_LIMIT)


def _dot(a, b):
    return jnp.dot(a, b, preferred_element_type=F32)


def _dot_nt(a, b):
    return lax.dot_general(a, b, (((1,), (1,)), ((), ())), preferred_element_type=F32)


def _dot_tn(a, b):
    return lax.dot_general(a, b, (((0,), (0,)), ((), ())), preferred_element_type=F32)


def _split3(x):
    hi = x.astype(BF16)
    r = x - hi.astype(F32)
    mid = r.astype(BF16)
    lo = (r - mid.astype(F32)).astype(BF16)
    return hi, mid, lo


def _sigmoid(x):
    return 1.0 / (1.0 + jnp.exp(-x))


def _log_sigmoid(x):
    return jnp.minimum(x, 0.0) - jnp.log(1.0 + jnp.exp(-jnp.abs(x)))


def _layer_norm(y, g, b):
    mu = jnp.mean(y, axis=-1, keepdims=True)
    yc = y - mu
    var = jnp.mean(yc * yc, axis=-1, keepdims=True)
    return yc * lax.rsqrt(var + LN_EPS) * g + b


def _mod_body(c_ref, w_ref, b_ref, o_ref):
    c = c_ref[...]
    a = c * _sigmoid(c)
    hi, mid, lo = _split3(a)
    w = w_ref[...]
    whi = w.astype(BF16)
    wlo = (w - whi.astype(F32)).astype(BF16)
    o_ref[...] = (_dot(hi, whi) + _dot(mid, whi) + _dot(hi, wlo) + _dot(lo, whi) + _dot(mid, wlo)) + b_ref[...]


def _mod_call(cc, w, b):
    n = w.shape[1]
    tn = 1536
    return pl.pallas_call(
        _mod_body,
        grid=(n // tn,),
        in_specs=[pl.BlockSpec((SUBLANES, D_MODEL), lambda j: (0, 0)),
                  pl.BlockSpec((D_MODEL, tn), lambda j: (0, j)),
                  pl.BlockSpec((1, tn), lambda j: (0, j))],
        out_specs=pl.BlockSpec((SUBLANES, tn), lambda j: (0, j)),
        out_shape=jax.ShapeDtypeStruct((SUBLANES, n), F32),
        compiler_params=_cparams(("arbitrary",)),
        name="mod",
    )(cc, w, b.reshape(1, n))


def _proj_body(x_ref, xp_ref, xn_ref, mod_ref, w_ref, wgt_ref, cw_ref, cb_ref, bgr_ref, bgc_ref,
               nq_ref, nk_ref, cos_ref, sin_ref,
               qm_ref, km_ref, vm_ref, om_ref, g_ref, gt_ref, qa_ref, ka_ref, va_ref,
               *, tl, tiles_per_seq, rope):
    t = lax.rem(pl.program_id(0), tiles_per_seq)
    shift = mod_ref[0, 0:1, :]
    scale1 = 1.0 + mod_ref[0, 1:2, :]

    def modulate(v):
        return (v * scale1 + shift).astype(BF16)

    h = modulate(x_ref[...])
    z = _dot(h, w_ref[...])
    wqk = w_ref[:, C_QK:C_VM]
    zp = _dot(modulate(xp_ref[...]), wqk)[SUBLANES - 1:SUBLANES, :]
    zn = _dot(modulate(xn_ref[...]), wqk)[0:1, :]
    zp = jnp.where(t != 0, zp, 0.0)
    zn = jnp.where(t != tiles_per_seq - 1, zn, 0.0)
    zqk = z[:, C_QK:C_VM]
    rows = lax.broadcasted_iota(jnp.int32, (tl, 1), 0)
    z_dn = jnp.where(rows == 0, zp, pltpu.roll(zqk, 1, 0))
    z_up = jnp.where(rows == tl - 1, zn, pltpu.roll(zqk, tl - 1, 0))
    y = cw_ref[0:1, :] * z_dn + cw_ref[1:2, :] * zqk + cw_ref[2:3, :] * z_up + cb_ref[...]
    y = y * _sigmoid(y)
    qm_ref[...] = (y[:, :M_W] * (M_DH ** -0.5)).astype(BF16)
    km_ref[...] = y[:, M_W:].astype(BF16)
    vm_ref[...] = z[:, C_VM:C_OM].astype(BF16)
    om_ref[...] = z[:, C_OM:C_G]
    g_ref[...] = z[:, C_G:C_QA] + bgr_ref[...]
    gt_ref[...] = _dot_nt(wgt_ref[...], h) + bgc_ref[...]

    lane = lax.broadcasted_iota(jnp.int32, (1, LANES), 1)
    if rope:
        cos = cos_ref[...]
        sin = sin_ref[...]
        lo = lax.rem(lane, 32) < 16

    def rope_fn(v):
        if not rope:
            return v
        partner = jnp.where(lo, pltpu.roll(v, LANES - 16, 1), pltpu.roll(v, 16, 1))
        return v * cos + partner * sin

    for hh in range(A_HEADS):
        zq = z[:, C_QA + LANES * hh:C_QA + LANES * (hh + 1)]
        ms = jnp.sum(zq * zq, axis=-1, keepdims=True) * (1.0 / A_DH)
        qn = zq * lax.rsqrt(ms + RMS_EPS) * nq_ref[...]
        qa_ref[hh] = (rope_fn(qn) * ATTN_SCALE).astype(BF16)
    zk = z[:, C_KA:C_VA]
    sq = zk * zk
    first = lane < A_DH
    s0 = jnp.sum(jnp.where(first, sq, 0.0), axis=-1, keepdims=True)
    s1 = jnp.sum(jnp.where(first, 0.0, sq), axis=-1, keepdims=True)
    ms = jnp.where(first, s0, s1) * (1.0 / A_DH)
    kn = zk * lax.rsqrt(ms + RMS_EPS) * nk_ref[...]
    ka_ref[...] = rope_fn(kn).astype(BF16)
    va_ref[...] = z[:, C_VA:C_END].astype(BF16)


def _proj_call(xf, mod2, w_all, wgt, cw, cb, bgr, bgc, nq, nk, cos, sin, *, seq_len, rope):
    n = xf.shape[0]
    tl = min(512, seq_len)
    tps = seq_len // tl
    nt = n // tl
    hb = tl // SUBLANES
    nblk8 = n // SUBLANES
    const = lambda i: (0, 0)
    body = functools.partial(_proj_body, tl=tl, tiles_per_seq=tps, rope=rope)
    out_shapes = (
        jax.ShapeDtypeStruct((n, M_W), BF16),
        jax.ShapeDtypeStruct((n, M_W), BF16),
        jax.ShapeDtypeStruct((n, M_W), BF16),
        jax.ShapeDtypeStruct((n, M_W), F32),
        jax.ShapeDtypeStruct((n, LANES), F32),
        jax.ShapeDtypeStruct((N_GATES, n), F32),
        jax.ShapeDtypeStruct((A_HEADS, n, LANES), BF16),
        jax.ShapeDtypeStruct((n, LANES), BF16),
        jax.ShapeDtypeStruct((n, LANES), BF16),
    )
    row = lambda i: (i, 0)
    out_specs = (
        pl.BlockSpec((tl, M_W), row), pl.BlockSpec((tl, M_W), row), pl.BlockSpec((tl, M_W), row),
        pl.BlockSpec((tl, M_W), row), pl.BlockSpec((tl, LANES), row),
        pl.BlockSpec((N_GATES, tl), lambda i: (0, i)),
        pl.BlockSpec((A_HEADS, tl, LANES), lambda i: (0, i, 0)),
        pl.BlockSpec((tl, LANES), row), pl.BlockSpec((tl, LANES), row),
    )
    in_specs = [
        pl.BlockSpec((tl, D_MODEL), row),
        pl.BlockSpec((SUBLANES, D_MODEL), lambda i: (jnp.maximum(i * hb - 1, 0), 0)),
        pl.BlockSpec((SUBLANES, D_MODEL), lambda i: (jnp.minimum((i + 1) * hb, nblk8 - 1), 0)),
        pl.BlockSpec((1, 2, D_MODEL), lambda i: (i // tps if mod2.shape[0] > 1 else 0, 0, 0)),
        pl.BlockSpec((D_MODEL, C_END), const),
        pl.BlockSpec((N_GATES, D_MODEL), const),
        pl.BlockSpec((3, 2 * M_W), const),
        pl.BlockSpec((1, 2 * M_W), const),
        pl.BlockSpec((1, LANES), const),
        pl.BlockSpec((N_GATES, 1), const),
        pl.BlockSpec((1, LANES), const),
        pl.BlockSpec((1, LANES), const),
        pl.BlockSpec((tl, LANES), lambda i: (lax.rem(i, tps), 0)),
        pl.BlockSpec((tl, LANES), lambda i: (lax.rem(i, tps), 0)),
    ]
    return pl.pallas_call(
        body, grid=(nt,), in_specs=in_specs, out_specs=out_specs, out_shape=out_shapes,
        compiler_params=_cparams(("arbitrary",)), name="proj_rope" if rope else "proj_ctx",
    )(xf, xf, xf, mod2, w_all, wgt, cw, cb, bgr, bgc, nq, nk, cos, sin)


def _mlstm_body(qf_ref, kf_ref, vf_ref, gf_ref, gtf_ref, qb_ref, kb_ref, vb_ref, gb_ref, gtb_ref,
                ct0_ref, m0_ref, hf_ref, hb_ref, cto_ref, mo_ref, ct_s, m_s):
    c = pl.program_id(1)
    nc = pl.num_programs(1)
    T = M_CHUNK

    @pl.when(c == 0)
    def _():
        ct_s[...] = ct0_ref[0]
        m_s[...] = m0_ref[0]

    ri = lax.broadcasted_iota(jnp.int32, (T, T), 0)
    ci = lax.broadcasted_iota(jnp.int32, (T, T), 1)
    le = ri <= ci
    ge = ri >= ci
    u_le = jnp.where(le, 1.0, 0.0).astype(BF16)
    u_ge = jnp.where(ge, 1.0, 0.0).astype(BF16)
    ones_col = jnp.where(ci == 0, 1.0, 0.0).astype(BF16)

    dirs = ((qf_ref, kf_ref, vf_ref, gf_ref, gtf_ref, hf_ref, ge, u_le, u_ge),
            (qb_ref, kb_ref, vb_ref, gb_ref, gtb_ref, hb_ref, le, u_ge, u_le))
    for d, (q_ref, k_ref, v_ref, g_ref, gt_ref, h_ref, mask, u_row, u_col) in enumerate(dirs):
        grp = gt_ref[2 * M_HEADS * d:2 * M_HEADS * (d + 1), :]
        lf8 = _log_sigmoid(grp)
        a, b_, c_ = _split3(lf8)
        b8 = _dot(a, u_row) + _dot(b_, u_row) + _dot(c_, u_row)
        gcol = g_ref[...]
        a, b_, c_ = _split3(_log_sigmoid(gcol))
        bcol = _dot(u_col, a) + _dot(u_col, b_) + _dot(u_col, c_)
        for h in range(M_HEADS):
            ch = M_HEADS * d + h
            g_row = grp[h:h + 1, :] - b8[M_HEADS + h:M_HEADS + h + 1, :]
            li_col = gcol[:, 2 * M_HEADS * d + h:2 * M_HEADS * d + h + 1]
            b_col = bcol[:, 2 * M_HEADS * d + M_HEADS + h:2 * M_HEADS * d + M_HEADS + h + 1]
            g_col = li_col - b_col
            m_prev = m_s[ch:ch + 1, 0:1]
            d0 = jnp.where(mask, g_row, NEG_INF)
            mx = jnp.maximum(m_prev, jnp.max(d0, axis=1, keepdims=True))
            dmat = jnp.exp(d0 - mx)
            sl = slice(M_DH * h, M_DH * (h + 1))
            q = q_ref[:, sl]
            k = k_ref[:, sl]
            v = v_ref[:, sl]
            s = _dot_nt(q, k) * dmat
            vaug = jnp.concatenate([v, ones_col], axis=1)
            ct = ct_s[ch]
            num = _dot(s.astype(BF16), vaug) + jnp.exp(m_prev - mx) * _dot(q, ct.astype(BF16))
            den = num[:, M_DH:M_DH + 1]
            floor = jnp.exp(-(b_col + mx))
            h_ref[:, sl] = num[:, :M_DH] / jnp.maximum(jnp.abs(den), floor)
            mxl = jnp.maximum(m_prev, jnp.max(g_row, axis=1, keepdims=True))
            b_last = jnp.sum(lf8[M_HEADS + h:M_HEADS + h + 1, :], axis=1, keepdims=True)
            w_col = jnp.exp(g_col - mxl)
            wv = (w_col * vaug.astype(F32)).astype(BF16)
            ct_s[ch] = jnp.exp(m_prev - mxl) * ct + _dot_tn(k, wv)
            m_s[ch:ch + 1, :] = jnp.broadcast_to(b_last + mxl, (1, LANES))

    @pl.when(c == nc - 1)
    def _():
        cto_ref[0] = ct_s[...]
        mo_ref[0] = m_s[...]


def _mlstm_call(qm, km, vm, g, gt, ct0, m0, *, batch, seq_len):
    n = qm.shape[0]
    nc = seq_len // M_CHUNK
    nch = 2 * M_HEADS
    fwd = lambda b, c: (b * nc + c, 0)
    bwd = lambda b, c: (b * nc + nc - 1 - c, 0)
    fwd_t = lambda b, c: (0, b * nc + c)
    bwd_t = lambda b, c: (0, b * nc + nc - 1 - c)
    tile = lambda im: pl.BlockSpec((M_CHUNK, M_W), im)
    in_specs = [tile(fwd), tile(fwd), tile(fwd), pl.BlockSpec((M_CHUNK, LANES), fwd), pl.BlockSpec((N_GATES, M_CHUNK), fwd_t),
                tile(bwd), tile(bwd), tile(bwd), pl.BlockSpec((M_CHUNK, LANES), bwd), pl.BlockSpec((N_GATES, M_CHUNK), bwd_t),
                pl.BlockSpec((1, nch, M_DH, 2 * M_DH), lambda b, c: (b, 0, 0, 0)),
                pl.BlockSpec((1, nch, LANES), lambda b, c: (b, 0, 0))]
    out_specs = (tile(fwd), tile(bwd),
                 pl.BlockSpec((1, nch, M_DH, 2 * M_DH), lambda b, c: (b, 0, 0, 0)),
                 pl.BlockSpec((1, nch, LANES), lambda b, c: (b, 0, 0)))
    out_shape = (jax.ShapeDtypeStruct((n, M_W), F32), jax.ShapeDtypeStruct((n, M_W), F32),
                 jax.ShapeDtypeStruct((batch, nch, M_DH, 2 * M_DH), F32),
                 jax.ShapeDtypeStruct((batch, nch, LANES), F32))
    return pl.pallas_call(
        _mlstm_body, grid=(batch, nc), in_specs=in_specs, out_specs=out_specs, out_shape=out_shape,
        scratch_shapes=[pltpu.VMEM((nch, M_DH, 2 * M_DH), F32), pltpu.VMEM((nch, LANES), F32)],
        compiler_params=_cparams(("arbitrary", "arbitrary")), name="mlstm",
    )(qm, km, vm, g, gt, qm, km, vm, g, gt, ct0, m0)


def _attn_body(q_ref, k_ref, v_ref, o_ref, m_s, acc_s, *, tq, tk, n_keys):
    m_s[...] = jnp.full(m_s.shape, NEG_INF, F32)
    acc_s[...] = jnp.zeros(acc_s.shape, F32)
    ci = lax.broadcasted_iota(jnp.int32, (tk, LANES), 1)
    ones_col = jnp.where(ci == 0, 1.0, 0.0).astype(BF16)

    def step(j, carry):
        off = pl.multiple_of(j * tk, tk)
        kt = k_ref[0, pl.ds(off, tk), :]
        vaug = jnp.concatenate([v_ref[0, pl.ds(off, tk), :], ones_col], axis=1)
        for h in range(A_HEADS):
            s = _dot_nt(q_ref[h], kt)
            m_old = m_s[h]
            m_new = jnp.maximum(m_old, jnp.max(s, axis=1, keepdims=True))
            p = jnp.exp(s - m_new)
            acc_s[h] = jnp.exp(m_old - m_new) * acc_s[h] + _dot(p.astype(BF16), vaug)
            m_s[h] = m_new
        return carry

    lax.fori_loop(0, n_keys // tk, step, 0)
    for h in range(A_HEADS):
        acc = acc_s[h]
        o_ref[h] = (acc[:, :LANES] / acc[:, LANES:LANES + 1]).astype(BF16)


def _attn_call(qa, k_all, v_all, *, batch, seq_len):
    n = qa.shape[1]
    n_keys = k_all.shape[1]
    tq = 256
    tk = 768 if n_keys % 768 == 0 else 256
    nq = seq_len // tq
    body = functools.partial(_attn_body, tq=tq, tk=tk, n_keys=n_keys)
    return pl.pallas_call(
        body, grid=(batch, nq),
        in_specs=[pl.BlockSpec((A_HEADS, tq, LANES), lambda b, i: (0, b * nq + i, 0)),
                  pl.BlockSpec((1, n_keys, LANES), lambda b, i: (b, 0, 0)),
                  pl.BlockSpec((1, n_keys, LANES), lambda b, i: (b, 0, 0))],
        out_specs=pl.BlockSpec((A_HEADS, tq, LANES), lambda b, i: (0, b * nq + i, 0)),
        out_shape=jax.ShapeDtypeStruct((A_HEADS, n, LANES), BF16),
        scratch_shapes=[pltpu.VMEM((A_HEADS, tq, 1), F32), pltpu.VMEM((A_HEADS, tq, 2 * LANES), F32)],
        compiler_params=_cparams(("arbitrary", "arbitrary")), name="attn",
    )(qa, k_all, v_all)


def _mix_body(hf_ref, hb_ref, om_ref, att_ref, x_ref, mod_ref, wom_ref, woa_ref, mhw_ref, lg_ref, lb_ref,
              wq_ref, keys_ref, x1_ref, st_ref):
    hsum = hf_ref[...] + hb_ref[...]
    om = om_ref[...]
    parts = []
    for h in range(M_HEADS):
        sl = slice(M_DH * h, M_DH * (h + 1))
        blk = hsum[:, sl]
        mu = jnp.mean(blk, axis=-1, keepdims=True)
        xc = blk - mu
        var = jnp.mean(xc * xc, axis=-1, keepdims=True)
        hn = xc * lax.rsqrt(var + LN_EPS) * mhw_ref[:, sl]
        parts.append((_sigmoid(om[:, sl]) * hn).astype(BF16))
    mix = _dot(jnp.concatenate(parts, axis=1), wom_ref[...])
    for h in range(A_HEADS):
        mix = mix + _dot(att_ref[h], woa_ref[h])
    g1 = mod_ref[0, 0:1, :]
    sh2 = mod_ref[0, 1:2, :]
    sc2 = mod_ref[0, 2:3, :]
    x1 = _layer_norm(DEEPNORM_ALPHA * x_ref[...] + g1 * mix, lg_ref[...], lb_ref[...])
    x1_ref[...] = x1
    hp = (x1 * (1.0 + sc2) + sh2).astype(BF16)
    qp = _dot(hp, wq_ref[...])
    for j in range(2 * P_HEADS):
        blk = qp[:, P_HALF * j:P_HALF * (j + 1)].astype(BF16)
        st_ref[j] = _dot_nt(keys_ref[j], blk)


def _mix_call(hf, hb, om, att, xf, mod3, wom, woa, mhw, lg, lb, wq, keys, *, seq_len):
    n = xf.shape[0]
    tl = 256
    tps = seq_len // tl
    row = lambda i: (i, 0)
    const2 = lambda i: (0, 0)
    const3 = lambda i: (0, 0, 0)
    return pl.pallas_call(
        _mix_body, grid=(n // tl,),
        in_specs=[pl.BlockSpec((tl, M_W), row), pl.BlockSpec((tl, M_W), row), pl.BlockSpec((tl, M_W), row),
                  pl.BlockSpec((A_HEADS, tl, LANES), lambda i: (0, i, 0)),
                  pl.BlockSpec((tl, D_MODEL), row),
                  pl.BlockSpec((1, 3, D_MODEL), lambda i: (i // tps, 0, 0)),
                  pl.BlockSpec((M_W, D_MODEL), const2),
                  pl.BlockSpec((A_HEADS, LANES, D_MODEL), const3),
                  pl.BlockSpec((1, M_W), const2), pl.BlockSpec((1, D_MODEL), const2), pl.BlockSpec((1, D_MODEL), const2),
                  pl.BlockSpec((D_MODEL, P_HEADS * P_QDIM), const2),
                  pl.BlockSpec((2 * P_HEADS, N_KEYS, P_HALF), const3)],
        out_specs=(pl.BlockSpec((tl, D_MODEL), row),
                   pl.BlockSpec((2 * P_HEADS, N_KEYS, tl), lambda i: (0, 0, i))),
        out_shape=(jax.ShapeDtypeStruct((n, D_MODEL), F32),
                   jax.ShapeDtypeStruct((2 * P_HEADS, N_KEYS, n), F32)),
        compiler_params=_cparams(("arbitrary",)), name="mix",
    )(hf, hb, om, att, xf, mod3, wom, woa, mhw, lg, lb, wq, keys)


_CAND_ROWS = tuple(P_TOPK // (i + 1) for i in range(P_TOPK))


def _top16(s):
    v = s
    rank = jnp.full(s.shape, float(N_KEYS), F32)
    vals = []
    for i in range(P_TOPK):
        m = jnp.max(v, axis=0, keepdims=True)
        hit = v == m
        rank = jnp.where(hit, float(i), rank)
        v = jnp.where(hit, NEG_INF, v)
        vals.append(m)
    return rank, vals


def _topk_body(st_ref, r2_ref, e2_ref, c1_ref, e1_ref):
    for h in range(P_HEADS):
        s1 = st_ref[2 * h]
        s2 = st_ref[2 * h + 1]
        rank1, v1 = _top16(s1)
        rank2, v2 = _top16(s2)
        s2v8 = jnp.concatenate(v2[:SUBLANES], axis=0)
        s2v16 = jnp.concatenate(v2, axis=0)
        row8 = lax.broadcasted_iota(jnp.int32, s2v8.shape, 0)
        cands = []
        for i, lim in enumerate(_CAND_ROWS):
            if i == 0:
                cands.append(v1[0] + s2v16)
            else:
                cands.append(jnp.where(row8 < lim, v1[i] + s2v8, NEG_INF))
        work = list(cands)
        tau = None
        for it in range(P_TOPK):
            m = jnp.max(work[0], axis=0, keepdims=True)
            for w in work[1:]:
                m = jnp.maximum(m, jnp.max(w, axis=0, keepdims=True))
            if it == P_TOPK - 1:
                tau = m
            else:
                work = [jnp.where(w == m, NEG_INF, w) for w in work]
        top = v1[0] + v2[0]
        z = jnp.zeros_like(tau)
        c1 = jnp.zeros_like(rank1)
        for i, cnd in enumerate(cands):
            sel = cnd >= tau
            cnt = jnp.sum(jnp.where(sel, 1.0, 0.0), axis=0, keepdims=True)
            z = z + jnp.sum(jnp.where(sel, jnp.exp(cnd - top), 0.0), axis=0, keepdims=True)
            c1 = jnp.where(rank1 == float(i), cnt, c1)
        r2_ref[h] = rank2
        c1_ref[h] = c1
        e1_ref[h] = jnp.exp(s1 - v1[0]) / z
        e2_ref[h] = jnp.exp(s2 - v2[0])


def _topk_call(st):
    n = st.shape[2]
    tt = 256
    spec_in = pl.BlockSpec((2 * P_HEADS, N_KEYS, tt), lambda i: (0, 0, i))
    spec_out = pl.BlockSpec((P_HEADS, N_KEYS, tt), lambda i: (0, 0, i))
    shp = jax.ShapeDtypeStruct((P_HEADS, N_KEYS, n), F32)
    return pl.pallas_call(
        _topk_body, grid=(n // tt,), in_specs=[spec_in], out_specs=(spec_out,) * 4, out_shape=(shp,) * 4,
        compiler_params=_cparams(("arbitrary",)), name="topk",
    )(st)


def _peer_body(x1_ref, mod_ref, u_ref, v_ref, r2_ref, e2_ref, c1_ref, e1_ref, lg_ref, lb_ref,
               o_ref, hp_s, ht_s, acc_s, *, et):
    e = pl.program_id(1)

    @pl.when(e == 0)
    def _():
        sh2 = mod_ref[0, 0:1, :]
        sc2 = mod_ref[0, 1:2, :]
        hp_s[...] = (x1_ref[...] * (1.0 + sc2) + sh2).astype(BF16)
        acc_s[...] = jnp.zeros(acc_s.shape, F32)

    at = _dot_nt(u_ref[...], hp_s[...])
    for ai in range(et // N_KEYS):
        a_blk = at[N_KEYS * ai:N_KEYS * (ai + 1), :]
        w = jnp.zeros_like(a_blk)
        for h in range(P_HEADS):
            c1row = c1_ref[h, ai:ai + 1, :]
            e1row = e1_ref[h, ai:ai + 1, :]
            w = w + jnp.where(r2_ref[h] < c1row, e2_ref[h], 0.0) * e1row
        act = 0.5 * a_blk * (1.0 + lax.erf(a_blk * (2.0 ** -0.5)))
        ht_s[N_KEYS * ai:N_KEYS * (ai + 1), :] = (act * w).astype(BF16)
    acc_s[...] += _dot_tn(ht_s[...], v_ref[...])

    @pl.when(e == pl.num_programs(1) - 1)
    def _():
        g2 = mod_ref[0, 2:3, :]
        y = DEEPNORM_ALPHA * x1_ref[...] + g2 * acc_s[...]
        o_ref[...] = _layer_norm(y, lg_ref[...], lb_ref[...])


def _peer_call(x1, mod3, u, v, r2, e2, c1, e1, lg, lb, *, seq_len):
    n = x1.shape[0]
    n_exp = u.shape[0]
    tt = 512
    et = 1024
    tps = seq_len // tt
    na = et // N_KEYS
    body = functools.partial(_peer_body, et=et)
    tok = lambda t, e: (t, 0)
    const2 = lambda t, e: (0, 0)
    full = pl.BlockSpec((P_HEADS, N_KEYS, tt), lambda t, e: (0, 0, t))
    part = pl.BlockSpec((P_HEADS, na, tt), lambda t, e: (0, e, t))
    return pl.pallas_call(
        body, grid=(n // tt, n_exp // et),
        in_specs=[pl.BlockSpec((tt, D_MODEL), tok),
                  pl.BlockSpec((1, 3, D_MODEL), lambda t, e: (t // tps, 0, 0)),
                  pl.BlockSpec((et, D_MODEL), lambda t, e: (e, 0)),
                  pl.BlockSpec((et, D_MODEL), lambda t, e: (e, 0)),
                  full, full, part, part,
                  pl.BlockSpec((1, D_MODEL), const2), pl.BlockSpec((1, D_MODEL), const2)],
        out_specs=pl.BlockSpec((tt, D_MODEL), tok),
        out_shape=jax.ShapeDtypeStruct((n, D_MODEL), F32),
        scratch_shapes=[pltpu.VMEM((tt, D_MODEL), BF16), pltpu.VMEM((et, tt), BF16), pltpu.VMEM((tt, D_MODEL), F32)],
        compiler_params=_cparams(("arbitrary", "arbitrary")), name="peer",
    )(x1, mod3, u, v, r2, e2, c1, e1, lg, lb)


def _rope_tables(n_tok):
    pos = jnp.arange(n_tok, dtype=jnp.int32)
    row = (pos // GRID_W).astype(F32)
    col = (pos % GRID_W).astype(F32)
    d = jnp.arange(A_DH)
    freq = ROPE_THETA ** (-(d % 16).astype(F32) / 16.0)
    p = jnp.where(d[None, :] < A_DH // 2, row[:, None], col[:, None])
    ang = p * freq[None, :]
    sign = jnp.where((d % 32) < 16, -1.0, 1.0)
    cos = jnp.tile(jnp.cos(ang), (1, LANES // A_DH))
    sin = jnp.tile(jnp.sin(ang) * sign[None, :], (1, LANES // A_DH))
    return cos, sin


def _proj_weights(w_in, b_gates, q_norm_w, k_norm_w):
    pad_g = jnp.zeros((D_MODEL, LANES - N_GATES), F32)
    wq = w_in[:, G_END:G_END + A_W].reshape(D_MODEL, A_HEADS, 1, A_DH)
    grp = (jnp.arange(A_HEADS)[:, None] // A_REP == jnp.arange(A_KV)[None, :]).astype(F32)
    wq_pad = (wq * grp[None, :, :, None]).reshape(D_MODEL, A_HEADS * LANES)
    w_all = jnp.concatenate([w_in[:, :4 * M_W], w_in[:, 4 * M_W:G_END], pad_g, wq_pad,
                             w_in[:, G_END + A_W:G_END + A_W + A_KVW], w_in[:, G_END + A_W + A_KVW:]], axis=1).astype(BF16)
    wgt = w_in[:, 4 * M_W:G_END].T.astype(BF16)
    bgr = jnp.concatenate([b_gates, jnp.zeros((LANES - N_GATES,), F32)]).reshape(1, LANES)
    bgc = b_gates.reshape(N_GATES, 1)
    nq = jnp.tile(q_norm_w, LANES // A_DH).reshape(1, LANES)
    nk = jnp.tile(k_norm_w, LANES // A_DH).reshape(1, LANES)
    return w_all, wgt, bgr, bgc, nq, nk


def _forward(x, c, ctx, c_ctx, w_mod, b_mod, w_in, conv_w, conv_b, b_gates, mh_norm_w, q_norm_w, k_norm_w,
             w_out, ln1_g, ln1_b, peer_wq, peer_keys, peer_u, peer_v, ln2_g, ln2_b):
    batch, seq_len, _ = x.shape
    ctx_len = ctx.shape[1]
    l = 0
    D = D_MODEL

    cc = jnp.zeros((SUBLANES, D), F32).at[:batch].set(c).at[batch].set(c_ctx)
    mod = _mod_call(cc, w_mod[l], b_mod[l])
    modl = mod[:batch].reshape(batch, N_MOD, D)
    modc = mod[batch:batch + 1].reshape(1, N_MOD, D)

    w_all, wgt, bgr, bgc, nq, nk = _proj_weights(w_in[l], b_gates[l], q_norm_w[l], k_norm_w[l])
    cw = conv_w[l]
    cb = conv_b[l].reshape(1, 2 * M_W)
    cos, sin = _rope_tables(seq_len)

    xf = x.reshape(batch * seq_len, D)
    cf = ctx.reshape(batch * ctx_len, D)
    proj = functools.partial(_proj_call, w_all=w_all, wgt=wgt, cw=cw, cb=cb, bgr=bgr, bgc=bgc, nq=nq, nk=nk)
    ctl = min(512, ctx_len)
    (qm_c, km_c, vm_c, _, g_c, gt_c, _, ka_c, va_c) = proj(
        cf, modc[:, 0:2], cos=cos[:ctl], sin=sin[:ctl], seq_len=ctx_len, rope=False)
    (qm, km, vm, om, g, gt, qa, ka, va) = proj(xf, modl[:, 0:2], cos=cos, sin=sin, seq_len=seq_len, rope=True)

    nch = 2 * M_HEADS
    ct0 = jnp.zeros((batch, nch, M_DH, 2 * M_DH), F32)
    m0 = jnp.zeros((batch, nch, LANES), F32)
    _, _, ct_c, m_c = _mlstm_call(qm_c, km_c, vm_c, g_c, gt_c, ct0, m0, batch=batch, seq_len=ctx_len)
    hf, hb, _, _ = _mlstm_call(qm, km, vm, g, gt, ct_c, m_c, batch=batch, seq_len=seq_len)

    k_all = jnp.concatenate([ka.reshape(batch, seq_len, LANES), ka_c.reshape(batch, ctx_len, LANES)], axis=1)
    v_all = jnp.concatenate([va.reshape(batch, seq_len, LANES), va_c.reshape(batch, ctx_len, LANES)], axis=1)
    att = _attn_call(qa, k_all, v_all, batch=batch, seq_len=seq_len)

    wo = w_out[l]
    wom = wo[:M_W].astype(BF16)
    woa = wo[M_W:].reshape(A_HEADS, 1, A_DH, D)
    grp = (jnp.arange(A_HEADS)[:, None] // A_REP == jnp.arange(A_KV)[None, :]).astype(F32)
    woa = (woa * grp[:, :, None, None]).reshape(A_HEADS, LANES, D).astype(BF16)
    mod_mix = jnp.stack([modl[:, 2], modl[:, 3], modl[:, 4]], axis=1)
    keys = peer_keys[l].reshape(2 * P_HEADS, N_KEYS, P_HALF).astype(BF16)
    x1, st = _mix_call(hf, hb, om, att, xf, mod_mix, wom, woa, mh_norm_w[l].reshape(1, M_W),
                       ln1_g[l].reshape(1, D), ln1_b[l].reshape(1, D), peer_wq[l].astype(BF16), keys, seq_len=seq_len)

    r2, e2, c1, e1 = _topk_call(st)
    mod_peer = jnp.stack([modl[:, 3], modl[:, 4], modl[:, 5]], axis=1)
    out = _peer_call(x1, mod_peer, peer_u[l].astype(BF16), peer_v[l].astype(BF16), r2, e2, c1, e1,
                     ln2_g[l].reshape(1, D), ln2_b[l].reshape(1, D), seq_len=seq_len)
    return out.reshape(batch, seq_len, D)


def kernel(x, c, ctx, c_ctx, w_mod, b_mod, w_in, conv_w, conv_b, b_gates, mh_norm_w, q_norm_w, k_norm_w, w_out,
           ln1_g, ln1_b, peer_wq, peer_keys, peer_u, peer_v, ln2_g, ln2_b):
    return _forward(x, c, ctx, c_ctx, w_mod, b_mod, w_in, conv_w, conv_b, b_gates, mh_norm_w, q_norm_w, k_norm_w,
                    w_out, ln1_g, ln1_b, peer_wq, peer_keys, peer_u, peer_v, ln2_g, ln2_b)
```

```python
import functools

import jax
import jax.numpy as jnp
from jax import lax
from jax.experimental import pallas as pl
from jax.experimental.pallas import tpu as pltpu

F32 = jnp.float32
BF16 = jnp.bfloat16

D_MODEL = 1024
DEPTH = 1
GRID_W = 64
M_HEADS = 4
M_DH = 128
M_W = M_HEADS * M_DH
M_CHUNK = 128
N_GATES = 4 * M_HEADS
A_HEADS = 8
A_KV = 2
A_REP = A_HEADS // A_KV
A_DH = 64
A_W = A_HEADS * A_DH
A_KVW = A_KV * A_DH
ROPE_THETA = 10000.0
ATTN_SCALE = A_DH ** -0.5
G_END = 4 * M_W + N_GATES
P_HEADS = 8
N_KEYS = 128
P_HALF = 128
P_QDIM = 2 * P_HALF
P_TOPK = 16
N_MOD = 6
DEEPNORM_ALPHA = (2.0 * DEPTH) ** 0.25
LN_EPS = 1e-5
RMS_EPS = 1e-6

LANES = 128
SUBLANES = 8
VMEM_LIMIT = 56 * 1024 * 1024

C_QK = 0
C_VM = 2 * M_W
C_OM = 3 * M_W
C_G = 4 * M_W
C_QA = C_G + LANES
C_KA = C_QA + A_HEADS * LANES
C_VA = C_KA + LANES
C_END = C_VA + LANES

NEG_INF = float("-inf")


def _cparams(sem):
    return pltpu.CompilerParams(dimension_semantics=sem, vmem_limit_bytes=VMEM_LIMIT)


def _dot(a, b):
    return jnp.dot(a, b, preferred_element_type=F32)


def _dot_nt(a, b):
    return lax.dot_general(a, b, (((1,), (1,)), ((), ())), preferred_element_type=F32)


def _dot_tn(a, b):
    return lax.dot_general(a, b, (((0,), (0,)), ((), ())), preferred_element_type=F32)


def _split3(x):
    hi = x.astype(BF16)
    r = x - hi.astype(F32)
    mid = r.astype(BF16)
    lo = (r - mid.astype(F32)).astype(BF16)
    return hi, mid, lo


def _sigmoid(x):
    return 1.0 / (1.0 + jnp.exp(-x))


def _log_sigmoid(x):
    return jnp.minimum(x, 0.0) - jnp.log(1.0 + jnp.exp(-jnp.abs(x)))


def _layer_norm(y, g, b):
    mu = jnp.mean(y, axis=-1, keepdims=True)
    yc = y - mu
    var = jnp.mean(yc * yc, axis=-1, keepdims=True)
    return yc * lax.rsqrt(var + LN_EPS) * g + b


def _mod_body(c_ref, w_ref, b_ref, o_ref):
    c = c_ref[...]
    a = c * _sigmoid(c)
    hi, mid, lo = _split3(a)
    w = w_ref[...]
    whi = w.astype(BF16)
    wlo = (w - whi.astype(F32)).astype(BF16)
    o_ref[...] = (_dot(hi, whi) + _dot(mid, whi) + _dot(hi, wlo) + _dot(lo, whi) + _dot(mid, wlo)) + b_ref[...]


def _mod_call(cc, w, b):
    n = w.shape[1]
    tn = 1536
    return pl.pallas_call(
        _mod_body,
        grid=(n // tn,),
        in_specs=[pl.BlockSpec((SUBLANES, D_MODEL), lambda j: (0, 0)),
                  pl.BlockSpec((D_MODEL, tn), lambda j: (0, j)),
                  pl.BlockSpec((1, tn), lambda j: (0, j))],
        out_specs=pl.BlockSpec((SUBLANES, tn), lambda j: (0, j)),
        out_shape=jax.ShapeDtypeStruct((SUBLANES, n), F32),
        compiler_params=_cparams(("arbitrary",)),
        name="mod",
    )(cc, w, b.reshape(1, n))


def _proj_body(x_ref, xp_ref, xn_ref, mod_ref, w_ref, wgt_ref, cw_ref, cb_ref, bgr_ref, bgc_ref,
               nq_ref, nk_ref, cos_ref, sin_ref,
               qm_ref, km_ref, vm_ref, om_ref, g_ref, gt_ref, qa_ref, ka_ref, va_ref,
               *, tl, tiles_per_seq, rope):
    t = lax.rem(pl.program_id(0), tiles_per_seq)
    shift = mod_ref[0, 0:1, :]
    scale1 = 1.0 + mod_ref[0, 1:2, :]

    def modulate(v):
        return (v * scale1 + shift).astype(BF16)

    h = modulate(x_ref[...])
    z = _dot(h, w_ref[...])
    wqk = w_ref[:, C_QK:C_VM]
    zp = _dot(modulate(xp_ref[...]), wqk)[SUBLANES - 1:SUBLANES, :]
    zn = _dot(modulate(xn_ref[...]), wqk)[0:1, :]
    zp = jnp.where(t != 0, zp, 0.0)
    zn = jnp.where(t != tiles_per_seq - 1, zn, 0.0)
    zqk = z[:, C_QK:C_VM]
    rows = lax.broadcasted_iota(jnp.int32, (tl, 1), 0)
    z_dn = jnp.where(rows == 0, zp, pltpu.roll(zqk, 1, 0))
    z_up = jnp.where(rows == tl - 1, zn, pltpu.roll(zqk, tl - 1, 0))
    y = cw_ref[0:1, :] * z_dn + cw_ref[1:2, :] * zqk + cw_ref[2:3, :] * z_up + cb_ref[...]
    y = y * _sigmoid(y)
    qm_ref[...] = (y[:, :M_W] * (M_DH ** -0.5)).astype(BF16)
    km_ref[...] = y[:, M_W:].astype(BF16)
    vm_ref[...] = z[:, C_VM:C_OM].astype(BF16)
    om_ref[...] = z[:, C_OM:C_G]
    g_ref[...] = z[:, C_G:C_QA] + bgr_ref[...]
    gt_ref[...] = _dot_nt(wgt_ref[...], h) + bgc_ref[...]

    lane = lax.broadcasted_iota(jnp.int32, (1, LANES), 1)
    if rope:
        cos = cos_ref[...]
        sin = sin_ref[...]
        lo = lax.rem(lane, 32) < 16

    def rope_fn(v):
        if not rope:
            return v
        partner = jnp.where(lo, pltpu.roll(v, LANES - 16, 1), pltpu.roll(v, 16, 1))
        return v * cos + partner * sin

    for hh in range(A_HEADS):
        zq = z[:, C_QA + LANES * hh:C_QA + LANES * (hh + 1)]
        ms = jnp.sum(zq * zq, axis=-1, keepdims=True) * (1.0 / A_DH)
        qn = zq * lax.rsqrt(ms + RMS_EPS) * nq_ref[...]
        qa_ref[hh] = (rope_fn(qn) * ATTN_SCALE).astype(BF16)
    zk = z[:, C_KA:C_VA]
    sq = zk * zk
    first = lane < A_DH
    s0 = jnp.sum(jnp.where(first, sq, 0.0), axis=-1, keepdims=True)
    s1 = jnp.sum(jnp.where(first, 0.0, sq), axis=-1, keepdims=True)
    ms = jnp.where(first, s0, s1) * (1.0 / A_DH)
    kn = zk * lax.rsqrt(ms + RMS_EPS) * nk_ref[...]
    ka_ref[...] = rope_fn(kn).astype(BF16)
    va_ref[...] = z[:, C_VA:C_END].astype(BF16)


def _proj_call(xf, mod2, w_all, wgt, cw, cb, bgr, bgc, nq, nk, cos, sin, *, seq_len, rope):
    n = xf.shape[0]
    tl = min(512, seq_len)
    tps = seq_len // tl
    nt = n // tl
    hb = tl // SUBLANES
    nblk8 = n // SUBLANES
    const = lambda i: (0, 0)
    body = functools.partial(_proj_body, tl=tl, tiles_per_seq=tps, rope=rope)
    out_shapes = (
        jax.ShapeDtypeStruct((n, M_W), BF16),
        jax.ShapeDtypeStruct((n, M_W), BF16),
        jax.ShapeDtypeStruct((n, M_W), BF16),
        jax.ShapeDtypeStruct((n, M_W), F32),
        jax.ShapeDtypeStruct((n, LANES), F32),
        jax.ShapeDtypeStruct((N_GATES, n), F32),
        jax.ShapeDtypeStruct((A_HEADS, n, LANES), BF16),
        jax.ShapeDtypeStruct((n, LANES), BF16),
        jax.ShapeDtypeStruct((n, LANES), BF16),
    )
    row = lambda i: (i, 0)
    out_specs = (
        pl.BlockSpec((tl, M_W), row), pl.BlockSpec((tl, M_W), row), pl.BlockSpec((tl, M_W), row),
        pl.BlockSpec((tl, M_W), row), pl.BlockSpec((tl, LANES), row),
        pl.BlockSpec((N_GATES, tl), lambda i: (0, i)),
        pl.BlockSpec((A_HEADS, tl, LANES), lambda i: (0, i, 0)),
        pl.BlockSpec((tl, LANES), row), pl.BlockSpec((tl, LANES), row),
    )
    in_specs = [
        pl.BlockSpec((tl, D_MODEL), row),
        pl.BlockSpec((SUBLANES, D_MODEL), lambda i: (jnp.maximum(i * hb - 1, 0), 0)),
        pl.BlockSpec((SUBLANES, D_MODEL), lambda i: (jnp.minimum((i + 1) * hb, nblk8 - 1), 0)),
        pl.BlockSpec((1, 2, D_MODEL), lambda i: (i // tps if mod2.shape[0] > 1 else 0, 0, 0)),
        pl.BlockSpec((D_MODEL, C_END), const),
        pl.BlockSpec((N_GATES, D_MODEL), const),
        pl.BlockSpec((3, 2 * M_W), const),
        pl.BlockSpec((1, 2 * M_W), const),
        pl.BlockSpec((1, LANES), const),
        pl.BlockSpec((N_GATES, 1), const),
        pl.BlockSpec((1, LANES), const),
        pl.BlockSpec((1, LANES), const),
        pl.BlockSpec((tl, LANES), lambda i: (lax.rem(i, tps), 0)),
        pl.BlockSpec((tl, LANES), lambda i: (lax.rem(i, tps), 0)),
    ]
    return pl.pallas_call(
        body, grid=(nt,), in_specs=in_specs, out_specs=out_specs, out_shape=out_shapes,
        compiler_params=_cparams(("arbitrary",)), name="proj_rope" if rope else "proj_ctx",
    )(xf, xf, xf, mod2, w_all, wgt, cw, cb, bgr, bgc, nq, nk, cos, sin)


def _mlstm_body(qf_ref, kf_ref, vf_ref, gf_ref, gtf_ref, qb_ref, kb_ref, vb_ref, gb_ref, gtb_ref,
                ct0_ref, m0_ref, hf_ref, hb_ref, cto_ref, mo_ref, ct_s, m_s):
    c = pl.program_id(1)
    nc = pl.num_programs(1)
    T = M_CHUNK

    @pl.when(c == 0)
    def _():
        ct_s[...] = ct0_ref[0]
        m_s[...] = m0_ref[0]

    ri = lax.broadcasted_iota(jnp.int32, (T, T), 0)
    ci = lax.broadcasted_iota(jnp.int32, (T, T), 1)
    le = ri <= ci
    ge = ri >= ci
    u_le = jnp.where(le, 1.0, 0.0).astype(BF16)
    u_ge = jnp.where(ge, 1.0, 0.0).astype(BF16)
    ones_col = jnp.where(ci == 0, 1.0, 0.0).astype(BF16)

    dirs = ((qf_ref, kf_ref, vf_ref, gf_ref, gtf_ref, hf_ref, ge, u_le, u_ge),
            (qb_ref, kb_ref, vb_ref, gb_ref, gtb_ref, hb_ref, le, u_ge, u_le))
    for d, (q_ref, k_ref, v_ref, g_ref, gt_ref, h_ref, mask, u_row, u_col) in enumerate(dirs):
        grp = gt_ref[2 * M_HEADS * d:2 * M_HEADS * (d + 1), :]
        lf8 = _log_sigmoid(grp)
        a, b_, c_ = _split3(lf8)
        b8 = _dot(a, u_row) + _dot(b_, u_row) + _dot(c_, u_row)
        gcol = g_ref[...]
        a, b_, c_ = _split3(_log_sigmoid(gcol))
        bcol = _dot(u_col, a) + _dot(u_col, b_) + _dot(u_col, c_)
        for h in range(M_HEADS):
            ch = M_HEADS * d + h
            g_row = grp[h:h + 1, :] - b8[M_HEADS + h:M_HEADS + h + 1, :]
            li_col = gcol[:, 2 * M_HEADS * d + h:2 * M_HEADS * d + h + 1]
            b_col = bcol[:, 2 * M_HEADS * d + M_HEADS + h:2 * M_HEADS * d + M_HEADS + h + 1]
            g_col = li_col - b_col
            m_prev = m_s[ch:ch + 1, 0:1]
            d0 = jnp.where(mask, g_row, NEG_INF)
            mx = jnp.maximum(m_prev, jnp.max(d0, axis=1, keepdims=True))
            dmat = jnp.exp(d0 - mx)
            sl = slice(M_DH * h, M_DH * (h + 1))
            q = q_ref[:, sl]
            k = k_ref[:, sl]
            v = v_ref[:, sl]
            s = _dot_nt(q, k) * dmat
            vaug = jnp.concatenate([v, ones_col], axis=1)
            ct = ct_s[ch]
            num = _dot(s.astype(BF16), vaug) + jnp.exp(m_prev - mx) * _dot(q, ct.astype(BF16))
            den = num[:, M_DH:M_DH + 1]
            floor = jnp.exp(-(b_col + mx))
            h_ref[:, sl] = num[:, :M_DH] / jnp.maximum(jnp.abs(den), floor)
            mxl = jnp.maximum(m_prev, jnp.max(g_row, axis=1, keepdims=True))
            b_last = jnp.sum(lf8[M_HEADS + h:M_HEADS + h + 1, :], axis=1, keepdims=True)
            w_col = jnp.exp(g_col - mxl)
            wv = (w_col * vaug.astype(F32)).astype(BF16)
            ct_s[ch] = jnp.exp(m_prev - mxl) * ct + _dot_tn(k, wv)
            m_s[ch:ch + 1, :] = jnp.broadcast_to(b_last + mxl, (1, LANES))

    @pl.when(c == nc - 1)
    def _():
        cto_ref[0] = ct_s[...]
        mo_ref[0] = m_s[...]


def _mlstm_call(qm, km, vm, g, gt, ct0, m0, *, batch, seq_len):
    n = qm.shape[0]
    nc = seq_len // M_CHUNK
    nch = 2 * M_HEADS
    fwd = lambda b, c: (b * nc + c, 0)
    bwd = lambda b, c: (b * nc + nc - 1 - c, 0)
    fwd_t = lambda b, c: (0, b * nc + c)
    bwd_t = lambda b, c: (0, b * nc + nc - 1 - c)
    tile = lambda im: pl.BlockSpec((M_CHUNK, M_W), im)
    in_specs = [tile(fwd), tile(fwd), tile(fwd), pl.BlockSpec((M_CHUNK, LANES), fwd), pl.BlockSpec((N_GATES, M_CHUNK), fwd_t),
                tile(bwd), tile(bwd), tile(bwd), pl.BlockSpec((M_CHUNK, LANES), bwd), pl.BlockSpec((N_GATES, M_CHUNK), bwd_t),
                pl.BlockSpec((1, nch, M_DH, 2 * M_DH), lambda b, c: (b, 0, 0, 0)),
                pl.BlockSpec((1, nch, LANES), lambda b, c: (b, 0, 0))]
    out_specs = (tile(fwd), tile(bwd),
                 pl.BlockSpec((1, nch, M_DH, 2 * M_DH), lambda b, c: (b, 0, 0, 0)),
                 pl.BlockSpec((1, nch, LANES), lambda b, c: (b, 0, 0)))
    out_shape = (jax.ShapeDtypeStruct((n, M_W), F32), jax.ShapeDtypeStruct((n, M_W), F32),
                 jax.ShapeDtypeStruct((batch, nch, M_DH, 2 * M_DH), F32),
                 jax.ShapeDtypeStruct((batch, nch, LANES), F32))
    return pl.pallas_call(
        _mlstm_body, grid=(batch, nc), in_specs=in_specs, out_specs=out_specs, out_shape=out_shape,
        scratch_shapes=[pltpu.VMEM((nch, M_DH, 2 * M_DH), F32), pltpu.VMEM((nch, LANES), F32)],
        compiler_params=_cparams(("arbitrary", "arbitrary")), name="mlstm",
    )(qm, km, vm, g, gt, qm, km, vm, g, gt, ct0, m0)


def _attn_body(q_ref, k_ref, v_ref, o_ref, m_s, acc_s, *, tq, tk, n_keys):
    m_s[...] = jnp.full(m_s.shape, NEG_INF, F32)
    acc_s[...] = jnp.zeros(acc_s.shape, F32)
    ci = lax.broadcasted_iota(jnp.int32, (tk, LANES), 1)
    ones_col = jnp.where(ci == 0, 1.0, 0.0).astype(BF16)

    def step(j, carry):
        off = pl.multiple_of(j * tk, tk)
        kt = k_ref[0, pl.ds(off, tk), :]
        vaug = jnp.concatenate([v_ref[0, pl.ds(off, tk), :], ones_col], axis=1)
        for h in range(A_HEADS):
            s = _dot_nt(q_ref[h], kt)
            m_old = m_s[h]
            m_new = jnp.maximum(m_old, jnp.max(s, axis=1, keepdims=True))
            p = jnp.exp(s - m_new)
            acc_s[h] = jnp.exp(m_old - m_new) * acc_s[h] + _dot(p.astype(BF16), vaug)
            m_s[h] = m_new
        return carry

    lax.fori_loop(0, n_keys // tk, step, 0)
    for h in range(A_HEADS):
        acc = acc_s[h]
        o_ref[h] = (acc[:, :LANES] / acc[:, LANES:LANES + 1]).astype(BF16)


def _attn_call(qa, k_all, v_all, *, batch, seq_len):
    n = qa.shape[1]
    n_keys = k_all.shape[1]
    tq = 256
    tk = 768 if n_keys % 768 == 0 else 256
    nq = seq_len // tq
    body = functools.partial(_attn_body, tq=tq, tk=tk, n_keys=n_keys)
    return pl.pallas_call(
        body, grid=(batch, nq),
        in_specs=[pl.BlockSpec((A_HEADS, tq, LANES), lambda b, i: (0, b * nq + i, 0)),
                  pl.BlockSpec((1, n_keys, LANES), lambda b, i: (b, 0, 0)),
                  pl.BlockSpec((1, n_keys, LANES), lambda b, i: (b, 0, 0))],
        out_specs=pl.BlockSpec((A_HEADS, tq, LANES), lambda b, i: (0, b * nq + i, 0)),
        out_shape=jax.ShapeDtypeStruct((A_HEADS, n, LANES), BF16),
        scratch_shapes=[pltpu.VMEM((A_HEADS, tq, 1), F32), pltpu.VMEM((A_HEADS, tq, 2 * LANES), F32)],
        compiler_params=_cparams(("arbitrary", "arbitrary")), name="attn",
    )(qa, k_all, v_all)


def _mix_body(hf_ref, hb_ref, om_ref, att_ref, x_ref, mod_ref, wom_ref, woa_ref, mhw_ref, lg_ref, lb_ref,
              wq_ref, keys_ref, x1_ref, st_ref):
    hsum = hf_ref[...] + hb_ref[...]
    om = om_ref[...]
    parts = []
    for h in range(M_HEADS):
        sl = slice(M_DH * h, M_DH * (h + 1))
        blk = hsum[:, sl]
        mu = jnp.mean(blk, axis=-1, keepdims=True)
        xc = blk - mu
        var = jnp.mean(xc * xc, axis=-1, keepdims=True)
        hn = xc * lax.rsqrt(var + LN_EPS) * mhw_ref[:, sl]
        parts.append((_sigmoid(om[:, sl]) * hn).astype(BF16))
    mix = _dot(jnp.concatenate(parts, axis=1), wom_ref[...])
    for h in range(A_HEADS):
        mix = mix + _dot(att_ref[h], woa_ref[h])
    g1 = mod_ref[0, 0:1, :]
    sh2 = mod_ref[0, 1:2, :]
    sc2 = mod_ref[0, 2:3, :]
    x1 = _layer_norm(DEEPNORM_ALPHA * x_ref[...] + g1 * mix, lg_ref[...], lb_ref[...])
    x1_ref[...] = x1
    hp = (x1 * (1.0 + sc2) + sh2).astype(BF16)
    qp = _dot(hp, wq_ref[...])
    for j in range(2 * P_HEADS):
        blk = qp[:, P_HALF * j:P_HALF * (j + 1)].astype(BF16)
        st_ref[j] = _dot_nt(keys_ref[j], blk)


def _mix_call(hf, hb, om, att, xf, mod3, wom, woa, mhw, lg, lb, wq, keys, *, seq_len):
    n = xf.shape[0]
    tl = 256
    tps = seq_len // tl
    row = lambda i: (i, 0)
    const2 = lambda i: (0, 0)
    const3 = lambda i: (0, 0, 0)
    return pl.pallas_call(
        _mix_body, grid=(n // tl,),
        in_specs=[pl.BlockSpec((tl, M_W), row), pl.BlockSpec((tl, M_W), row), pl.BlockSpec((tl, M_W), row),
                  pl.BlockSpec((A_HEADS, tl, LANES), lambda i: (0, i, 0)),
                  pl.BlockSpec((tl, D_MODEL), row),
                  pl.BlockSpec((1, 3, D_MODEL), lambda i: (i // tps, 0, 0)),
                  pl.BlockSpec((M_W, D_MODEL), const2),
                  pl.BlockSpec((A_HEADS, LANES, D_MODEL), const3),
                  pl.BlockSpec((1, M_W), const2), pl.BlockSpec((1, D_MODEL), const2), pl.BlockSpec((1, D_MODEL), const2),
                  pl.BlockSpec((D_MODEL, P_HEADS * P_QDIM), const2),
                  pl.BlockSpec((2 * P_HEADS, N_KEYS, P_HALF), const3)],
        out_specs=(pl.BlockSpec((tl, D_MODEL), row),
                   pl.BlockSpec((2 * P_HEADS, N_KEYS, tl), lambda i: (0, 0, i))),
        out_shape=(jax.ShapeDtypeStruct((n, D_MODEL), F32),
                   jax.ShapeDtypeStruct((2 * P_HEADS, N_KEYS, n), F32)),
        compiler_params=_cparams(("arbitrary",)), name="mix",
    )(hf, hb, om, att, xf, mod3, wom, woa, mhw, lg, lb, wq, keys)


_CAND_ROWS = tuple(P_TOPK // (i + 1) for i in range(P_TOPK))


def _top16(s):
    v = s
    rank = jnp.full(s.shape, float(N_KEYS), F32)
    vals = []
    for i in range(P_TOPK):
        m = jnp.max(v, axis=0, keepdims=True)
        hit = v == m
        rank = jnp.where(hit, float(i), rank)
        v = jnp.where(hit, NEG_INF, v)
        vals.append(m)
    return rank, vals


def _topk_body(st_ref, r2_ref, e2_ref, c1_ref, e1_ref):
    for h in range(P_HEADS):
        s1 = st_ref[2 * h]
        s2 = st_ref[2 * h + 1]
        rank1, v1 = _top16(s1)
        rank2, v2 = _top16(s2)
        s2v8 = jnp.concatenate(v2[:SUBLANES], axis=0)
        s2v16 = jnp.concatenate(v2, axis=0)
        row8 = lax.broadcasted_iota(jnp.int32, s2v8.shape, 0)
        cands = []
        for i, lim in enumerate(_CAND_ROWS):
            if i == 0:
                cands.append(v1[0] + s2v16)
            else:
                cands.append(jnp.where(row8 < lim, v1[i] + s2v8, NEG_INF))
        work = list(cands)
        tau = None
        for it in range(P_TOPK):
            m = jnp.max(work[0], axis=0, keepdims=True)
            for w in work[1:]:
                m = jnp.maximum(m, jnp.max(w, axis=0, keepdims=True))
            if it == P_TOPK - 1:
                tau = m
            else:
                work = [jnp.where(w == m, NEG_INF, w) for w in work]
        top = v1[0] + v2[0]
        z = jnp.zeros_like(tau)
        c1 = jnp.zeros_like(rank1)
        for i, cnd in enumerate(cands):
            sel = cnd >= tau
            cnt = jnp.sum(jnp.where(sel, 1.0, 0.0), axis=0, keepdims=True)
            z = z + jnp.sum(jnp.where(sel, jnp.exp(cnd - top), 0.0), axis=0, keepdims=True)
            c1 = jnp.where(rank1 == float(i), cnt, c1)
        r2_ref[h] = pltpu.bitcast(rank2.astype(BF16), jnp.uint32)
        c1_ref[h] = c1
        e1_ref[h] = jnp.exp(s1 - v1[0]) / z
        e2_ref[h] = pltpu.bitcast(jnp.exp(s2 - v2[0]).astype(BF16), jnp.uint32)


def _topk_call(st):
    n = st.shape[2]
    tt = 256
    spec_in = pl.BlockSpec((2 * P_HEADS, N_KEYS, tt), lambda i: (0, 0, i))
    spec_out = pl.BlockSpec((P_HEADS, N_KEYS, tt), lambda i: (0, 0, i))
    shp = jax.ShapeDtypeStruct((P_HEADS, N_KEYS, n), F32)
    shp16 = jax.ShapeDtypeStruct((P_HEADS, N_KEYS // 2, n), jnp.uint32)
    spec16 = pl.BlockSpec((P_HEADS, N_KEYS // 2, tt), lambda i: (0, 0, i))
    return pl.pallas_call(
        _topk_body, grid=(n // tt,), in_specs=[spec_in], out_specs=(spec16, spec16, spec_out, spec_out),
        out_shape=(shp16, shp16, shp, shp),
        compiler_params=_cparams(("arbitrary",)), name="topk",
    )(st)


PEER_CHUNK = 2 * N_KEYS
BF16_ROWS = 16


def _peer_gate_stage(at_ref, h_ref, r2_ref, e2_ref, c1_ref, e1_ref, a0, tt):
    for k in range(PEER_CHUNK // N_KEYS):
        for j in range(tt // LANES):
            cols = slice(LANES * j, LANES * (j + 1))
            c1b = [jnp.broadcast_to(c1_ref[h, a0 + k:a0 + k + 1, cols], (BF16_ROWS, LANES)).astype(BF16)
                   for h in range(P_HEADS)]
            e1b = [jnp.broadcast_to(e1_ref[h, a0 + k:a0 + k + 1, cols], (BF16_ROWS, LANES)).astype(BF16)
                   for h in range(P_HEADS)]
            for rb in range(N_KEYS // BF16_ROWS):
                words = slice(SUBLANES * rb, SUBLANES * (rb + 1))
                rows = slice(N_KEYS * k + BF16_ROWS * rb, N_KEYS * k + BF16_ROWS * (rb + 1))
                hrows = slice((N_KEYS * k) // 2 + SUBLANES * rb, (N_KEYS * k) // 2 + SUBLANES * (rb + 1))
                w = None
                for h in range(P_HEADS):
                    e2v = pltpu.bitcast(e2_ref[h, words, cols], BF16)
                    r2v = pltpu.bitcast(r2_ref[h, words, cols], BF16)
                    term = jnp.where(r2v < c1b[h], e2v, jnp.zeros_like(e2v)) * e1b[h]
                    w = term if w is None else w + term
                a = at_ref[rows, cols]
                act = 0.5 * a * (1.0 + lax.erf(a * (2.0 ** -0.5)))
                h_ref[hrows, cols] = pltpu.bitcast(act.astype(BF16) * w, jnp.uint32)


def _peer_body(x1_ref, mod_ref, u_ref, vt_ref, r2_ref, e2_ref, c1_ref, e1_ref, lg_ref, lb_ref,
               o_ref, hpt_s, acc_s, at0_s, at1_s, h0_s, h1_s, *, et, tt):
    e = pl.program_id(1)

    @pl.when(e == 0)
    def _():
        sh2 = mod_ref[0, 0:1, :]
        sc2 = mod_ref[0, 1:2, :]
        hpt_s[...] = (x1_ref[...] * (1.0 + sc2) + sh2).T.astype(BF16)
        acc_s[...] = jnp.zeros(acc_s.shape, F32)

    at_bufs = (at0_s, at1_s)
    h_bufs = (h0_s, h1_s)
    nch = et // PEER_CHUNK

    def rows(c):
        return slice(PEER_CHUNK * c, PEER_CHUNK * (c + 1))

    at_bufs[0][...] = _dot(u_ref[rows(0), :], hpt_s[...])
    for c in range(nch):
        if c + 1 < nch:
            at_bufs[(c + 1) % 2][...] = _dot(u_ref[rows(c + 1), :], hpt_s[...])
        _peer_gate_stage(at_bufs[c % 2], h_bufs[c % 2], r2_ref, e2_ref, c1_ref, e1_ref,
                         c * (PEER_CHUNK // N_KEYS), tt)
        acc_s[...] += _dot(vt_ref[:, rows(c)], pltpu.bitcast(h_bufs[c % 2][...], BF16))

    @pl.when(e == pl.num_programs(1) - 1)
    def _():
        g2 = mod_ref[0, 2:3, :]
        y = DEEPNORM_ALPHA * x1_ref[...] + g2 * acc_s[...].T
        o_ref[...] = _layer_norm(y, lg_ref[...], lb_ref[...])


def _peer_call(x1, mod3, u, v, r2, e2, c1, e1, lg, lb, *, seq_len):
    n = x1.shape[0]
    n_exp = u.shape[0]
    tt = min(1024, seq_len)
    et = 1024
    tps = seq_len // tt
    na = et // N_KEYS
    body = functools.partial(_peer_body, et=et, tt=tt)
    tok = lambda t, e: (t, 0)
    const2 = lambda t, e: (0, 0)
    full = pl.BlockSpec((P_HEADS, N_KEYS // 2, tt), lambda t, e: (0, 0, t))
    part = pl.BlockSpec((P_HEADS, na, tt), lambda t, e: (0, e, t))
    return pl.pallas_call(
        body, grid=(n // tt, n_exp // et),
        in_specs=[pl.BlockSpec((tt, D_MODEL), tok),
                  pl.BlockSpec((1, 3, D_MODEL), lambda t, e: (t // tps, 0, 0)),
                  pl.BlockSpec((et, D_MODEL), lambda t, e: (e, 0)),
                  pl.BlockSpec((D_MODEL, et), lambda t, e: (0, e)),
                  full, full, part, part,
                  pl.BlockSpec((1, D_MODEL), const2), pl.BlockSpec((1, D_MODEL), const2)],
        out_specs=pl.BlockSpec((tt, D_MODEL), tok),
        out_shape=jax.ShapeDtypeStruct((n, D_MODEL), F32),
        scratch_shapes=[pltpu.VMEM((D_MODEL, tt), BF16), pltpu.VMEM((D_MODEL, tt), F32),
                        pltpu.VMEM((PEER_CHUNK, tt), F32), pltpu.VMEM((PEER_CHUNK, tt), F32),
                        pltpu.VMEM((PEER_CHUNK // 2, tt), jnp.uint32), pltpu.VMEM((PEER_CHUNK // 2, tt), jnp.uint32)],
        compiler_params=_cparams(("arbitrary", "arbitrary")), name="peer",
    )(x1, mod3, u, v, r2, e2, c1, e1, lg, lb)


def _rope_tables(n_tok):
    pos = jnp.arange(n_tok, dtype=jnp.int32)
    row = (pos // GRID_W).astype(F32)
    col = (pos % GRID_W).astype(F32)
    d = jnp.arange(A_DH)
    freq = ROPE_THETA ** (-(d % 16).astype(F32) / 16.0)
    p = jnp.where(d[None, :] < A_DH // 2, row[:, None], col[:, None])
    ang = p * freq[None, :]
    sign = jnp.where((d % 32) < 16, -1.0, 1.0)
    cos = jnp.tile(jnp.cos(ang), (1, LANES // A_DH))
    sin = jnp.tile(jnp.sin(ang) * sign[None, :], (1, LANES // A_DH))
    return cos, sin


def _proj_weights(w_in, b_gates, q_norm_w, k_norm_w):
    pad_g = jnp.zeros((D_MODEL, LANES - N_GATES), F32)
    wq = w_in[:, G_END:G_END + A_W].reshape(D_MODEL, A_HEADS, 1, A_DH)
    grp = (jnp.arange(A_HEADS)[:, None] // A_REP == jnp.arange(A_KV)[None, :]).astype(F32)
    wq_pad = (wq * grp[None, :, :, None]).reshape(D_MODEL, A_HEADS * LANES)
    w_all = jnp.concatenate([w_in[:, :4 * M_W], w_in[:, 4 * M_W:G_END], pad_g, wq_pad,
                             w_in[:, G_END + A_W:G_END + A_W + A_KVW], w_in[:, G_END + A_W + A_KVW:]], axis=1).astype(BF16)
    wgt = w_in[:, 4 * M_W:G_END].T.astype(BF16)
    bgr = jnp.concatenate([b_gates, jnp.zeros((LANES - N_GATES,), F32)]).reshape(1, LANES)
    bgc = b_gates.reshape(N_GATES, 1)
    nq = jnp.tile(q_norm_w, LANES // A_DH).reshape(1, LANES)
    nk = jnp.tile(k_norm_w, LANES // A_DH).reshape(1, LANES)
    return w_all, wgt, bgr, bgc, nq, nk


def _forward(x, c, ctx, c_ctx, w_mod, b_mod, w_in, conv_w, conv_b, b_gates, mh_norm_w, q_norm_w, k_norm_w,
             w_out, ln1_g, ln1_b, peer_wq, peer_keys, peer_u, peer_v, ln2_g, ln2_b):
    batch, seq_len, _ = x.shape
    ctx_len = ctx.shape[1]
    l = 0
    D = D_MODEL

    cc = jnp.zeros((SUBLANES, D), F32).at[:batch].set(c).at[batch].set(c_ctx)
    mod = _mod_call(cc, w_mod[l], b_mod[l])
    modl = mod[:batch].reshape(batch, N_MOD, D)
    modc = mod[batch:batch + 1].reshape(1, N_MOD, D)

    w_all, wgt, bgr, bgc, nq, nk = _proj_weights(w_in[l], b_gates[l], q_norm_w[l], k_norm_w[l])
    cw = conv_w[l]
    cb = conv_b[l].reshape(1, 2 * M_W)
    cos, sin = _rope_tables(seq_len)

    xf = x.reshape(batch * seq_len, D)
    cf = ctx.reshape(batch * ctx_len, D)
    proj = functools.partial(_proj_call, w_all=w_all, wgt=wgt, cw=cw, cb=cb, bgr=bgr, bgc=bgc, nq=nq, nk=nk)
    ctl = min(512, ctx_len)
    (qm_c, km_c, vm_c, _, g_c, gt_c, _, ka_c, va_c) = proj(
        cf, modc[:, 0:2], cos=cos[:ctl], sin=sin[:ctl], seq_len=ctx_len, rope=False)
    (qm, km, vm, om, g, gt, qa, ka, va) = proj(xf, modl[:, 0:2], cos=cos, sin=sin, seq_len=seq_len, rope=True)

    nch = 2 * M_HEADS
    ct0 = jnp.zeros((batch, nch, M_DH, 2 * M_DH), F32)
    m0 = jnp.zeros((batch, nch, LANES), F32)
    _, _, ct_c, m_c = _mlstm_call(qm_c, km_c, vm_c, g_c, gt_c, ct0, m0, batch=batch, seq_len=ctx_len)
    hf, hb, _, _ = _mlstm_call(qm, km, vm, g, gt, ct_c, m_c, batch=batch, seq_len=seq_len)

    k_all = jnp.concatenate([ka.reshape(batch, seq_len, LANES), ka_c.reshape(batch, ctx_len, LANES)], axis=1)
    v_all = jnp.concatenate([va.reshape(batch, seq_len, LANES), va_c.reshape(batch, ctx_len, LANES)], axis=1)
    att = _attn_call(qa, k_all, v_all, batch=batch, seq_len=seq_len)

    wo = w_out[l]
    wom = wo[:M_W].astype(BF16)
    woa = wo[M_W:].reshape(A_HEADS, 1, A_DH, D)
    grp = (jnp.arange(A_HEADS)[:, None] // A_REP == jnp.arange(A_KV)[None, :]).astype(F32)
    woa = (woa * grp[:, :, None, None]).reshape(A_HEADS, LANES, D).astype(BF16)
    mod_mix = jnp.stack([modl[:, 2], modl[:, 3], modl[:, 4]], axis=1)
    keys = peer_keys[l].reshape(2 * P_HEADS, N_KEYS, P_HALF).astype(BF16)
    x1, st = _mix_call(hf, hb, om, att, xf, mod_mix, wom, woa, mh_norm_w[l].reshape(1, M_W),
                       ln1_g[l].reshape(1, D), ln1_b[l].reshape(1, D), peer_wq[l].astype(BF16), keys, seq_len=seq_len)

    r2, e2, c1, e1 = _topk_call(st)
    mod_peer = jnp.stack([modl[:, 3], modl[:, 4], modl[:, 5]], axis=1)
    out = _peer_call(x1, mod_peer, peer_u[l].astype(BF16), peer_v[l].T.astype(BF16), r2, e2, c1, e1,
                     ln2_g[l].reshape(1, D), ln2_b[l].reshape(1, D), seq_len=seq_len)
    return out.reshape(batch, seq_len, D)


def kernel(x, c, ctx, c_ctx, w_mod, b_mod, w_in, conv_w, conv_b, b_gates, mh_norm_w, q_norm_w, k_norm_w, w_out,
           ln1_g, ln1_b, peer_wq, peer_keys, peer_u, peer_v, ln2_g, ln2_b):
    return _forward(x, c, ctx, c_ctx, w_mod, b_mod, w_in, conv_w, conv_b, b_gates, mh_norm_w, q_norm_w, k_norm_w,
                    w_out, ln1_g, ln1_b, peer_wq, peer_keys, peer_u, peer_v, ln2_g, ln2_b)
```

```python
import functools

import jax
import jax.numpy as jnp
from jax import lax
from jax.experimental import pallas as pl
from jax.experimental.pallas import tpu as pltpu

F32 = jnp.float32
BF16 = jnp.bfloat16

D_MODEL = 1024
DEPTH = 1
GRID_W = 64
M_HEADS = 4
M_DH = 128
M_W = M_HEADS * M_DH
M_CHUNK = 128
N_GATES = 4 * M_HEADS
A_HEADS = 8
A_KV = 2
A_REP = A_HEADS // A_KV
A_DH = 64
A_W = A_HEADS * A_DH
A_KVW = A_KV * A_DH
ROPE_THETA = 10000.0
ATTN_SCALE = A_DH ** -0.5
G_END = 4 * M_W + N_GATES
P_HEADS = 8
N_KEYS = 128
P_HALF = 128
P_QDIM = 2 * P_HALF
P_TOPK = 16
N_MOD = 6
DEEPNORM_ALPHA = (2.0 * DEPTH) ** 0.25
LN_EPS = 1e-5
RMS_EPS = 1e-6

LANES = 128
SUBLANES = 8
VMEM_LIMIT = 56 * 1024 * 1024

C_QK = 0
C_VM = 2 * M_W
C_OM = 3 * M_W
C_G = 4 * M_W
C_QA = C_G + LANES
C_KA = C_QA + A_HEADS * LANES
C_VA = C_KA + LANES
C_END = C_VA + LANES

NEG_INF = float("-inf")


def _cparams(sem):
    return pltpu.CompilerParams(dimension_semantics=sem, vmem_limit_bytes=VMEM_LIMIT)


def _dot(a, b):
    return jnp.dot(a, b, preferred_element_type=F32)


def _dot_nt(a, b):
    return lax.dot_general(a, b, (((1,), (1,)), ((), ())), preferred_element_type=F32)


def _dot_tn(a, b):
    return lax.dot_general(a, b, (((0,), (0,)), ((), ())), preferred_element_type=F32)


def _split3(x):
    hi = x.astype(BF16)
    r = x - hi.astype(F32)
    mid = r.astype(BF16)
    lo = (r - mid.astype(F32)).astype(BF16)
    return hi, mid, lo


def _sigmoid(x):
    return 1.0 / (1.0 + jnp.exp(-x))


def _log_sigmoid(x):
    return jnp.minimum(x, 0.0) - jnp.log(1.0 + jnp.exp(-jnp.abs(x)))


def _layer_norm(y, g, b):
    mu = jnp.mean(y, axis=-1, keepdims=True)
    yc = y - mu
    var = jnp.mean(yc * yc, axis=-1, keepdims=True)
    return yc * lax.rsqrt(var + LN_EPS) * g + b


def _mod_body(c_ref, w_ref, b_ref, o_ref):
    c = c_ref[...]
    a = c * _sigmoid(c)
    hi, mid, lo = _split3(a)
    w = w_ref[...]
    whi = w.astype(BF16)
    wlo = (w - whi.astype(F32)).astype(BF16)
    o_ref[...] = (_dot(hi, whi) + _dot(mid, whi) + _dot(hi, wlo) + _dot(lo, whi) + _dot(mid, wlo)) + b_ref[...]


def _mod_call(cc, w, b):
    n = w.shape[1]
    tn = 1536
    return pl.pallas_call(
        _mod_body,
        grid=(n // tn,),
        in_specs=[pl.BlockSpec((SUBLANES, D_MODEL), lambda j: (0, 0)),
                  pl.BlockSpec((D_MODEL, tn), lambda j: (0, j)),
                  pl.BlockSpec((1, tn), lambda j: (0, j))],
        out_specs=pl.BlockSpec((SUBLANES, tn), lambda j: (0, j)),
        out_shape=jax.ShapeDtypeStruct((SUBLANES, n), F32),
        compiler_params=_cparams(("arbitrary",)),
        name="mod",
    )(cc, w, b.reshape(1, n))


def _proj_body(x_ref, xp_ref, xn_ref, mod_ref, w_ref, wgt_ref, cw_ref, cb_ref, bgr_ref, bgc_ref,
               nq_ref, nk_ref, cos_ref, sin_ref,
               qm_ref, km_ref, vm_ref, om_ref, g_ref, gt_ref, qa_ref, ka_ref, va_ref,
               *, tl, tiles_per_seq, rope):
    t = lax.rem(pl.program_id(0), tiles_per_seq)
    shift = mod_ref[0, 0:1, :]
    scale1 = 1.0 + mod_ref[0, 1:2, :]

    def modulate(v):
        return (v * scale1 + shift).astype(BF16)

    h = modulate(x_ref[...])
    z = _dot(h, w_ref[...])
    wqk = w_ref[:, C_QK:C_VM]
    zp = _dot(modulate(xp_ref[...]), wqk)[SUBLANES - 1:SUBLANES, :]
    zn = _dot(modulate(xn_ref[...]), wqk)[0:1, :]
    zp = jnp.where(t != 0, zp, 0.0)
    zn = jnp.where(t != tiles_per_seq - 1, zn, 0.0)
    zqk = z[:, C_QK:C_VM]
    rows = lax.broadcasted_iota(jnp.int32, (tl, 1), 0)
    z_dn = jnp.where(rows == 0, zp, pltpu.roll(zqk, 1, 0))
    z_up = jnp.where(rows == tl - 1, zn, pltpu.roll(zqk, tl - 1, 0))
    y = cw_ref[0:1, :] * z_dn + cw_ref[1:2, :] * zqk + cw_ref[2:3, :] * z_up + cb_ref[...]
    y = y * _sigmoid(y)
    qm_ref[...] = (y[:, :M_W] * (M_DH ** -0.5)).astype(BF16)
    km_ref[...] = y[:, M_W:].astype(BF16)
    vm_ref[...] = z[:, C_VM:C_OM].astype(BF16)
    om_ref[...] = z[:, C_OM:C_G]
    g_ref[...] = z[:, C_G:C_QA] + bgr_ref[...]
    gt_ref[...] = _dot_nt(wgt_ref[...], h) + bgc_ref[...]

    lane = lax.broadcasted_iota(jnp.int32, (1, LANES), 1)
    if rope:
        cos = cos_ref[...]
        sin = sin_ref[...]
        lo = lax.rem(lane, 32) < 16

    def rope_fn(v):
        if not rope:
            return v
        partner = jnp.where(lo, pltpu.roll(v, LANES - 16, 1), pltpu.roll(v, 16, 1))
        return v * cos + partner * sin

    for hh in range(A_HEADS):
        zq = z[:, C_QA + LANES * hh:C_QA + LANES * (hh + 1)]
        ms = jnp.sum(zq * zq, axis=-1, keepdims=True) * (1.0 / A_DH)
        qn = zq * lax.rsqrt(ms + RMS_EPS) * nq_ref[...]
        qa_ref[hh] = (rope_fn(qn) * ATTN_SCALE).astype(BF16)
    zk = z[:, C_KA:C_VA]
    sq = zk * zk
    first = lane < A_DH
    s0 = jnp.sum(jnp.where(first, sq, 0.0), axis=-1, keepdims=True)
    s1 = jnp.sum(jnp.where(first, 0.0, sq), axis=-1, keepdims=True)
    ms = jnp.where(first, s0, s1) * (1.0 / A_DH)
    kn = zk * lax.rsqrt(ms + RMS_EPS) * nk_ref[...]
    ka_ref[...] = rope_fn(kn).astype(BF16)
    va_ref[...] = z[:, C_VA:C_END].astype(BF16)


def _proj_call(xf, mod2, w_all, wgt, cw, cb, bgr, bgc, nq, nk, cos, sin, *, seq_len, rope):
    n = xf.shape[0]
    tl = min(512, seq_len)
    tps = seq_len // tl
    nt = n // tl
    hb = tl // SUBLANES
    nblk8 = n // SUBLANES
    const = lambda i: (0, 0)
    body = functools.partial(_proj_body, tl=tl, tiles_per_seq=tps, rope=rope)
    out_shapes = (
        jax.ShapeDtypeStruct((n, M_W), BF16),
        jax.ShapeDtypeStruct((n, M_W), BF16),
        jax.ShapeDtypeStruct((n, M_W), BF16),
        jax.ShapeDtypeStruct((n, M_W), F32),
        jax.ShapeDtypeStruct((n, LANES), F32),
        jax.ShapeDtypeStruct((N_GATES, n), F32),
        jax.ShapeDtypeStruct((A_HEADS, n, LANES), BF16),
        jax.ShapeDtypeStruct((n, LANES), BF16),
        jax.ShapeDtypeStruct((n, LANES), BF16),
    )
    row = lambda i: (i, 0)
    out_specs = (
        pl.BlockSpec((tl, M_W), row), pl.BlockSpec((tl, M_W), row), pl.BlockSpec((tl, M_W), row),
        pl.BlockSpec((tl, M_W), row), pl.BlockSpec((tl, LANES), row),
        pl.BlockSpec((N_GATES, tl), lambda i: (0, i)),
        pl.BlockSpec((A_HEADS, tl, LANES), lambda i: (0, i, 0)),
        pl.BlockSpec((tl, LANES), row), pl.BlockSpec((tl, LANES), row),
    )
    in_specs = [
        pl.BlockSpec((tl, D_MODEL), row),
        pl.BlockSpec((SUBLANES, D_MODEL), lambda i: (jnp.maximum(i * hb - 1, 0), 0)),
        pl.BlockSpec((SUBLANES, D_MODEL), lambda i: (jnp.minimum((i + 1) * hb, nblk8 - 1), 0)),
        pl.BlockSpec((1, 2, D_MODEL), lambda i: (i // tps if mod2.shape[0] > 1 else 0, 0, 0)),
        pl.BlockSpec((D_MODEL, C_END), const),
        pl.BlockSpec((N_GATES, D_MODEL), const),
        pl.BlockSpec((3, 2 * M_W), const),
        pl.BlockSpec((1, 2 * M_W), const),
        pl.BlockSpec((1, LANES), const),
        pl.BlockSpec((N_GATES, 1), const),
        pl.BlockSpec((1, LANES), const),
        pl.BlockSpec((1, LANES), const),
        pl.BlockSpec((tl, LANES), lambda i: (lax.rem(i, tps), 0)),
        pl.BlockSpec((tl, LANES), lambda i: (lax.rem(i, tps), 0)),
    ]
    return pl.pallas_call(
        body, grid=(nt,), in_specs=in_specs, out_specs=out_specs, out_shape=out_shapes,
        compiler_params=_cparams(("arbitrary",)), name="proj_rope" if rope else "proj_ctx",
    )(xf, xf, xf, mod2, w_all, wgt, cw, cb, bgr, bgc, nq, nk, cos, sin)


def _mlstm_body(qf_ref, kf_ref, vf_ref, gf_ref, gtf_ref, qb_ref, kb_ref, vb_ref, gb_ref, gtb_ref,
                ct0_ref, m0_ref, hf_ref, hb_ref, cto_ref, mo_ref, ct_s, m_s):
    c = pl.program_id(1)
    nc = pl.num_programs(1)
    T = M_CHUNK

    @pl.when(c == 0)
    def _():
        ct_s[...] = ct0_ref[0]
        m_s[...] = m0_ref[0]

    ri = lax.broadcasted_iota(jnp.int32, (T, T), 0)
    ci = lax.broadcasted_iota(jnp.int32, (T, T), 1)
    le = ri <= ci
    ge = ri >= ci
    u_le = jnp.where(le, 1.0, 0.0).astype(BF16)
    u_ge = jnp.where(ge, 1.0, 0.0).astype(BF16)
    ones_col = jnp.where(ci == 0, 1.0, 0.0).astype(BF16)

    dirs = ((qf_ref, kf_ref, vf_ref, gf_ref, gtf_ref, hf_ref, ge, u_le, u_ge),
            (qb_ref, kb_ref, vb_ref, gb_ref, gtb_ref, hb_ref, le, u_ge, u_le))
    for d, (q_ref, k_ref, v_ref, g_ref, gt_ref, h_ref, mask, u_row, u_col) in enumerate(dirs):
        grp = gt_ref[2 * M_HEADS * d:2 * M_HEADS * (d + 1), :]
        lf8 = _log_sigmoid(grp)
        a, b_, c_ = _split3(lf8)
        b8 = _dot(a, u_row) + _dot(b_, u_row) + _dot(c_, u_row)
        gcol = g_ref[...]
        a, b_, c_ = _split3(_log_sigmoid(gcol))
        bcol = _dot(u_col, a) + _dot(u_col, b_) + _dot(u_col, c_)
        for h in range(M_HEADS):
            ch = M_HEADS * d + h
            g_row = grp[h:h + 1, :] - b8[M_HEADS + h:M_HEADS + h + 1, :]
            li_col = gcol[:, 2 * M_HEADS * d + h:2 * M_HEADS * d + h + 1]
            b_col = bcol[:, 2 * M_HEADS * d + M_HEADS + h:2 * M_HEADS * d + M_HEADS + h + 1]
            g_col = li_col - b_col
            m_prev = m_s[ch:ch + 1, 0:1]
            d0 = jnp.where(mask, g_row, NEG_INF)
            mx = jnp.maximum(m_prev, jnp.max(d0, axis=1, keepdims=True))
            dmat = jnp.exp(d0 - mx)
            sl = slice(M_DH * h, M_DH * (h + 1))
            q = q_ref[:, sl]
            k = k_ref[:, sl]
            v = v_ref[:, sl]
            s = _dot_nt(q, k) * dmat
            vaug = jnp.concatenate([v, ones_col], axis=1)
            ct = ct_s[ch]
            num = _dot(s.astype(BF16), vaug) + jnp.exp(m_prev - mx) * _dot(q, ct.astype(BF16))
            den = num[:, M_DH:M_DH + 1]
            floor = jnp.exp(-(b_col + mx))
            h_ref[:, sl] = num[:, :M_DH] / jnp.maximum(jnp.abs(den), floor)
            mxl = jnp.maximum(m_prev, jnp.max(g_row, axis=1, keepdims=True))
            b_last = jnp.sum(lf8[M_HEADS + h:M_HEADS + h + 1, :], axis=1, keepdims=True)
            w_col = jnp.exp(g_col - mxl)
            wv = (w_col * vaug.astype(F32)).astype(BF16)
            ct_s[ch] = jnp.exp(m_prev - mxl) * ct + _dot_tn(k, wv)
            m_s[ch:ch + 1, :] = jnp.broadcast_to(b_last + mxl, (1, LANES))

    @pl.when(c == nc - 1)
    def _():
        cto_ref[0] = ct_s[...]
        mo_ref[0] = m_s[...]


def _mlstm_call(qm, km, vm, g, gt, ct0, m0, *, batch, seq_len):
    n = qm.shape[0]
    nc = seq_len // M_CHUNK
    nch = 2 * M_HEADS
    fwd = lambda b, c: (b * nc + c, 0)
    bwd = lambda b, c: (b * nc + nc - 1 - c, 0)
    fwd_t = lambda b, c: (0, b * nc + c)
    bwd_t = lambda b, c: (0, b * nc + nc - 1 - c)
    tile = lambda im: pl.BlockSpec((M_CHUNK, M_W), im)
    in_specs = [tile(fwd), tile(fwd), tile(fwd), pl.BlockSpec((M_CHUNK, LANES), fwd), pl.BlockSpec((N_GATES, M_CHUNK), fwd_t),
                tile(bwd), tile(bwd), tile(bwd), pl.BlockSpec((M_CHUNK, LANES), bwd), pl.BlockSpec((N_GATES, M_CHUNK), bwd_t),
                pl.BlockSpec((1, nch, M_DH, 2 * M_DH), lambda b, c: (b, 0, 0, 0)),
                pl.BlockSpec((1, nch, LANES), lambda b, c: (b, 0, 0))]
    out_specs = (tile(fwd), tile(bwd),
                 pl.BlockSpec((1, nch, M_DH, 2 * M_DH), lambda b, c: (b, 0, 0, 0)),
                 pl.BlockSpec((1, nch, LANES), lambda b, c: (b, 0, 0)))
    out_shape = (jax.ShapeDtypeStruct((n, M_W), F32), jax.ShapeDtypeStruct((n, M_W), F32),
                 jax.ShapeDtypeStruct((batch, nch, M_DH, 2 * M_DH), F32),
                 jax.ShapeDtypeStruct((batch, nch, LANES), F32))
    return pl.pallas_call(
        _mlstm_body, grid=(batch, nc), in_specs=in_specs, out_specs=out_specs, out_shape=out_shape,
        scratch_shapes=[pltpu.VMEM((nch, M_DH, 2 * M_DH), F32), pltpu.VMEM((nch, LANES), F32)],
        compiler_params=_cparams(("arbitrary", "arbitrary")), name="mlstm",
    )(qm, km, vm, g, gt, qm, km, vm, g, gt, ct0, m0)


def _attn_body(q_ref, k_ref, v_ref, o_ref, *, tk, n_keys):
    ci = lax.broadcasted_iota(jnp.int32, (tk, LANES), 1)
    ones_col = jnp.where(ci == 0, 1.0, 0.0).astype(BF16)
    m = [None] * A_HEADS
    acc = [None] * A_HEADS
    for j in range(n_keys // tk):
        kt = k_ref[0, tk * j:tk * (j + 1), :]
        vaug = jnp.concatenate([v_ref[0, tk * j:tk * (j + 1), :], ones_col], axis=1)
        for h in range(A_HEADS):
            s = _dot_nt(q_ref[h], kt)
            row_max = jnp.max(s, axis=1, keepdims=True)
            if j == 0:
                m_new = row_max
                acc[h] = _dot(jnp.exp(s - m_new).astype(BF16), vaug)
            else:
                m_new = jnp.maximum(m[h], row_max)
                acc[h] = jnp.exp(m[h] - m_new) * acc[h] + _dot(jnp.exp(s - m_new).astype(BF16), vaug)
            m[h] = m_new
    for h in range(A_HEADS):
        o_ref[h] = (acc[h][:, :LANES] / acc[h][:, LANES:LANES + 1]).astype(BF16)


def _attn_call(qa, k_all, v_all, *, batch, seq_len):
    n = qa.shape[1]
    n_keys = k_all.shape[1]
    tq = 256
    tk = next(t for t in (2816, 1408, 768, 256) if n_keys % t == 0)
    nq = seq_len // tq
    body = functools.partial(_attn_body, tk=tk, n_keys=n_keys)
    return pl.pallas_call(
        body, grid=(batch, nq),
        in_specs=[pl.BlockSpec((A_HEADS, tq, LANES), lambda b, i: (0, b * nq + i, 0)),
                  pl.BlockSpec((1, n_keys, LANES), lambda b, i: (b, 0, 0)),
                  pl.BlockSpec((1, n_keys, LANES), lambda b, i: (b, 0, 0))],
        out_specs=pl.BlockSpec((A_HEADS, tq, LANES), lambda b, i: (0, b * nq + i, 0)),
        out_shape=jax.ShapeDtypeStruct((A_HEADS, n, LANES), BF16),
        compiler_params=_cparams(("arbitrary", "arbitrary")), name="attn",
    )(qa, k_all, v_all)


def _mix_body(hf_ref, hb_ref, om_ref, att_ref, x_ref, mod_ref, wom_ref, woa_ref, mhw_ref, lg_ref, lb_ref,
              wq_ref, keys_ref, x1_ref, st_ref):
    hsum = hf_ref[...] + hb_ref[...]
    om = om_ref[...]
    parts = []
    for h in range(M_HEADS):
        sl = slice(M_DH * h, M_DH * (h + 1))
        blk = hsum[:, sl]
        mu = jnp.mean(blk, axis=-1, keepdims=True)
        xc = blk - mu
        var = jnp.mean(xc * xc, axis=-1, keepdims=True)
        hn = xc * lax.rsqrt(var + LN_EPS) * mhw_ref[:, sl]
        parts.append((_sigmoid(om[:, sl]) * hn).astype(BF16))
    mix = _dot(jnp.concatenate(parts, axis=1), wom_ref[...])
    for h in range(A_HEADS):
        mix = mix + _dot(att_ref[h], woa_ref[h])
    g1 = mod_ref[0, 0:1, :]
    sh2 = mod_ref[0, 1:2, :]
    sc2 = mod_ref[0, 2:3, :]
    x1 = _layer_norm(DEEPNORM_ALPHA * x_ref[...] + g1 * mix, lg_ref[...], lb_ref[...])
    x1_ref[...] = x1
    hp = (x1 * (1.0 + sc2) + sh2).astype(BF16)
    qp = _dot(hp, wq_ref[...])
    for j in range(2 * P_HEADS):
        blk = qp[:, P_HALF * j:P_HALF * (j + 1)].astype(BF16)
        st_ref[j] = _dot_nt(keys_ref[j], blk)


def _mix_call(hf, hb, om, att, xf, mod3, wom, woa, mhw, lg, lb, wq, keys, *, seq_len):
    n = xf.shape[0]
    tl = 256
    tps = seq_len // tl
    row = lambda i: (i, 0)
    const2 = lambda i: (0, 0)
    const3 = lambda i: (0, 0, 0)
    return pl.pallas_call(
        _mix_body, grid=(n // tl,),
        in_specs=[pl.BlockSpec((tl, M_W), row), pl.BlockSpec((tl, M_W), row), pl.BlockSpec((tl, M_W), row),
                  pl.BlockSpec((A_HEADS, tl, LANES), lambda i: (0, i, 0)),
                  pl.BlockSpec((tl, D_MODEL), row),
                  pl.BlockSpec((1, 3, D_MODEL), lambda i: (i // tps, 0, 0)),
                  pl.BlockSpec((M_W, D_MODEL), const2),
                  pl.BlockSpec((A_HEADS, LANES, D_MODEL), const3),
                  pl.BlockSpec((1, M_W), const2), pl.BlockSpec((1, D_MODEL), const2), pl.BlockSpec((1, D_MODEL), const2),
                  pl.BlockSpec((D_MODEL, P_HEADS * P_QDIM), const2),
                  pl.BlockSpec((2 * P_HEADS, N_KEYS, P_HALF), const3)],
        out_specs=(pl.BlockSpec((tl, D_MODEL), row),
                   pl.BlockSpec((2 * P_HEADS, N_KEYS, tl), lambda i: (0, 0, i))),
        out_shape=(jax.ShapeDtypeStruct((n, D_MODEL), F32),
                   jax.ShapeDtypeStruct((2 * P_HEADS, N_KEYS, n), F32)),
        compiler_params=_cparams(("arbitrary",)), name="mix",
    )(hf, hb, om, att, xf, mod3, wom, woa, mhw, lg, lb, wq, keys)


def _sublane_all(op, x):
    x = op(x, pltpu.roll(x, 4, 0))
    x = op(x, pltpu.roll(x, 2, 0))
    return op(x, pltpu.roll(x, 1, 0))


def _tree(op, xs):
    xs = list(xs)
    while len(xs) > 1:
        xs = [op(xs[i], xs[i + 1]) if i + 1 < len(xs) else xs[i] for i in range(0, len(xs), 2)]
    return xs[0]


def _extract16(blocks):
    v = list(blocks)
    rank = [None] * len(v)
    vals = []
    for i in range(P_TOPK):
        m = _sublane_all(jnp.maximum, _tree(jnp.maximum, v))
        vals.append(m)
        for kb in range(len(v)):
            hit = v[kb] == m
            rank[kb] = jnp.where(hit, float(i), float(N_KEYS) if rank[kb] is None else rank[kb])
            if i + 1 < P_TOPK:
                v[kb] = jnp.where(hit, NEG_INF, v[kb])
    return vals, rank


_LOW_RANKS = 5


def _topk_unit(st_ref, r2_ref, e2_ref, c1_ref, e1_ref, h, lanes):
    nkb = N_KEYS // SUBLANES
    s1 = [st_ref[2 * h, SUBLANES * kb:SUBLANES * (kb + 1), lanes] for kb in range(nkb)]
    s2 = [st_ref[2 * h + 1, SUBLANES * kb:SUBLANES * (kb + 1), lanes] for kb in range(nkb)]
    v1, rank1 = _extract16(s1)
    v2, rank2 = _extract16(s2)
    row = lax.broadcasted_iota(jnp.int32, v1[0].shape, 0)

    def column(vals, base):
        out = vals[base + SUBLANES - 1]
        for r in range(SUBLANES - 2, -1, -1):
            out = jnp.where(row == r, vals[base + r], out)
        return out

    s2lo = column(v2, 0)
    s2hi = column(v2, SUBLANES)
    s1hi = column(v1, 6)
    cands = [
        v1[0] + s2lo, v1[0] + s2hi, v1[1] + s2lo,
        jnp.where(row < 5, v1[2] + s2lo, v1[4] + pltpu.roll(s2lo, 5, 0)),
        jnp.where(row < 4, v1[3] + s2lo,
                  jnp.where(row < 6, v1[5] + pltpu.roll(s2lo, 4, 0), v1[6] + pltpu.roll(s2lo, 6, 0))),
        jnp.where(row < 2, v1[7] + s2lo, s1hi + v2[0]),
        jnp.where(row < 2, jnp.where(row == 0, v1[14], v1[15]) + v2[0], NEG_INF),
    ]
    work = list(cands)
    for it in range(P_TOPK):
        tau = _sublane_all(jnp.maximum, _tree(jnp.maximum, work))
        if it + 1 < P_TOPK:
            work = [jnp.where(w == tau, NEG_INF, w) for w in work]
    top = v1[0] + v2[0]
    sel = [c >= tau for c in cands]
    ind = [jnp.where(m, 1.0, 0.0) for m in sel]
    z = _sublane_all(jnp.add, _tree(jnp.add, [jnp.where(m, jnp.exp(c - top), 0.0) for m, c in zip(sel, cands)]))
    cnt = [None] * _LOW_RANKS
    cnt[0] = _sublane_all(jnp.add, ind[0] + ind[1])
    cnt[1] = _sublane_all(jnp.add, ind[2])
    cnt[2] = _sublane_all(jnp.add, jnp.where(row < 5, ind[3], 0.0))
    cnt[4] = _sublane_all(jnp.add, ind[3]) - cnt[2]
    cnt[3] = _sublane_all(jnp.add, jnp.where(row < 4, ind[4], 0.0))
    zinv = 1.0 / z
    for kb in range(nkb):
        rows = slice(SUBLANES * kb, SUBLANES * (kb + 1))
        c1 = jnp.where(s1[kb] + v2[0] >= tau, 1.0, 0.0) + jnp.where(s1[kb] + v2[1] >= tau, 1.0, 0.0)
        for i in range(_LOW_RANKS):
            c1 = jnp.where(rank1[kb] == float(i), cnt[i], c1)
        c1_ref[h, rows, lanes] = c1
        e1_ref[h, rows, lanes] = jnp.exp(s1[kb] - v1[0]) * zinv
    for pb in range(nkb // 2):
        rows = slice(SUBLANES * pb, SUBLANES * (pb + 1))
        r2 = jnp.concatenate([rank2[2 * pb], rank2[2 * pb + 1]], axis=0).astype(BF16)
        e2 = jnp.concatenate([jnp.exp(s2[2 * pb] - v2[0]), jnp.exp(s2[2 * pb + 1] - v2[0])], axis=0).astype(BF16)
        r2_ref[h, rows, lanes] = pltpu.bitcast(r2, jnp.uint32)
        e2_ref[h, rows, lanes] = pltpu.bitcast(e2, jnp.uint32)


def _topk_body(st_ref, r2_ref, e2_ref, c1_ref, e1_ref):
    for h in range(P_HEADS):
        for lt in range(st_ref.shape[2] // LANES):
            _topk_unit(st_ref, r2_ref, e2_ref, c1_ref, e1_ref, h, slice(LANES * lt, LANES * (lt + 1)))


def _topk_call(st):
    n = st.shape[2]
    tt = 256
    spec_in = pl.BlockSpec((2 * P_HEADS, N_KEYS, tt), lambda i: (0, 0, i))
    spec_out = pl.BlockSpec((P_HEADS, N_KEYS, tt), lambda i: (0, 0, i))
    shp = jax.ShapeDtypeStruct((P_HEADS, N_KEYS, n), F32)
    shp16 = jax.ShapeDtypeStruct((P_HEADS, N_KEYS // 2, n), jnp.uint32)
    spec16 = pl.BlockSpec((P_HEADS, N_KEYS // 2, tt), lambda i: (0, 0, i))
    return pl.pallas_call(
        _topk_body, grid=(n // tt,), in_specs=[spec_in], out_specs=(spec16, spec16, spec_out, spec_out),
        out_shape=(shp16, shp16, shp, shp),
        compiler_params=_cparams(("arbitrary",)), name="topk",
    )(st)


PEER_CHUNK = 2 * N_KEYS
BF16_ROWS = 16


def _peer_gate_stage(at_ref, h_ref, r2_ref, e2_ref, c1_ref, e1_ref, a0, tt):
    for k in range(PEER_CHUNK // N_KEYS):
        for j in range(tt // LANES):
            cols = slice(LANES * j, LANES * (j + 1))
            c1b = [jnp.broadcast_to(c1_ref[h, a0 + k:a0 + k + 1, cols], (BF16_ROWS, LANES)).astype(BF16)
                   for h in range(P_HEADS)]
            e1b = [jnp.broadcast_to(e1_ref[h, a0 + k:a0 + k + 1, cols], (BF16_ROWS, LANES)).astype(BF16)
                   for h in range(P_HEADS)]
            for rb in range(N_KEYS // BF16_ROWS):
                words = slice(SUBLANES * rb, SUBLANES * (rb + 1))
                rows = slice(N_KEYS * k + BF16_ROWS * rb, N_KEYS * k + BF16_ROWS * (rb + 1))
                hrows = slice((N_KEYS * k) // 2 + SUBLANES * rb, (N_KEYS * k) // 2 + SUBLANES * (rb + 1))
                w = None
                for h in range(P_HEADS):
                    e2v = pltpu.bitcast(e2_ref[h, words, cols], BF16)
                    r2v = pltpu.bitcast(r2_ref[h, words, cols], BF16)
                    term = jnp.where(r2v < c1b[h], e2v, jnp.zeros_like(e2v)) * e1b[h]
                    w = term if w is None else w + term
                a = at_ref[rows, cols]
                act = 0.5 * a * (1.0 + lax.erf(a * (2.0 ** -0.5)))
                h_ref[hrows, cols] = pltpu.bitcast(act.astype(BF16) * w, jnp.uint32)


def _peer_body(x1_ref, mod_ref, u_ref, vt_ref, r2_ref, e2_ref, c1_ref, e1_ref, lg_ref, lb_ref,
               o_ref, hpt_s, acc_s, at0_s, at1_s, h0_s, h1_s, *, et, tt):
    e = pl.program_id(1)

    @pl.when(e == 0)
    def _():
        sh2 = mod_ref[0, 0:1, :]
        sc2 = mod_ref[0, 1:2, :]
        hpt_s[...] = (x1_ref[...] * (1.0 + sc2) + sh2).T.astype(BF16)
        acc_s[...] = jnp.zeros(acc_s.shape, F32)

    at_bufs = (at0_s, at1_s)
    h_bufs = (h0_s, h1_s)
    nch = et // PEER_CHUNK

    def rows(c):
        return slice(PEER_CHUNK * c, PEER_CHUNK * (c + 1))

    at_bufs[0][...] = _dot(u_ref[rows(0), :], hpt_s[...])
    for c in range(nch):
        if c + 1 < nch:
            at_bufs[(c + 1) % 2][...] = _dot(u_ref[rows(c + 1), :], hpt_s[...])
        _peer_gate_stage(at_bufs[c % 2], h_bufs[c % 2], r2_ref, e2_ref, c1_ref, e1_ref,
                         c * (PEER_CHUNK // N_KEYS), tt)
        acc_s[...] += _dot(vt_ref[:, rows(c)], pltpu.bitcast(h_bufs[c % 2][...], BF16))

    @pl.when(e == pl.num_programs(1) - 1)
    def _():
        g2 = mod_ref[0, 2:3, :]
        y = DEEPNORM_ALPHA * x1_ref[...] + g2 * acc_s[...].T
        o_ref[...] = _layer_norm(y, lg_ref[...], lb_ref[...])


def _peer_call(x1, mod3, u, v, r2, e2, c1, e1, lg, lb, *, seq_len):
    n = x1.shape[0]
    n_exp = u.shape[0]
    tt = min(1024, seq_len)
    et = 1024
    tps = seq_len // tt
    na = et // N_KEYS
    body = functools.partial(_peer_body, et=et, tt=tt)
    tok = lambda t, e: (t, 0)
    const2 = lambda t, e: (0, 0)
    full = pl.BlockSpec((P_HEADS, N_KEYS // 2, tt), lambda t, e: (0, 0, t))
    part = pl.BlockSpec((P_HEADS, na, tt), lambda t, e: (0, e, t))
    return pl.pallas_call(
        body, grid=(n // tt, n_exp // et),
        in_specs=[pl.BlockSpec((tt, D_MODEL), tok),
                  pl.BlockSpec((1, 3, D_MODEL), lambda t, e: (t // tps, 0, 0)),
                  pl.BlockSpec((et, D_MODEL), lambda t, e: (e, 0)),
                  pl.BlockSpec((D_MODEL, et), lambda t, e: (0, e)),
                  full, full, part, part,
                  pl.BlockSpec((1, D_MODEL), const2), pl.BlockSpec((1, D_MODEL), const2)],
        out_specs=pl.BlockSpec((tt, D_MODEL), tok),
        out_shape=jax.ShapeDtypeStruct((n, D_MODEL), F32),
        scratch_shapes=[pltpu.VMEM((D_MODEL, tt), BF16), pltpu.VMEM((D_MODEL, tt), F32),
                        pltpu.VMEM((PEER_CHUNK, tt), F32), pltpu.VMEM((PEER_CHUNK, tt), F32),
                        pltpu.VMEM((PEER_CHUNK // 2, tt), jnp.uint32), pltpu.VMEM((PEER_CHUNK // 2, tt), jnp.uint32)],
        compiler_params=_cparams(("arbitrary", "arbitrary")), name="peer",
    )(x1, mod3, u, v, r2, e2, c1, e1, lg, lb)


def _rope_tables(n_tok):
    pos = jnp.arange(n_tok, dtype=jnp.int32)
    row = (pos // GRID_W).astype(F32)
    col = (pos % GRID_W).astype(F32)
    d = jnp.arange(A_DH)
    freq = ROPE_THETA ** (-(d % 16).astype(F32) / 16.0)
    p = jnp.where(d[None, :] < A_DH // 2, row[:, None], col[:, None])
    ang = p * freq[None, :]
    sign = jnp.where((d % 32) < 16, -1.0, 1.0)
    cos = jnp.tile(jnp.cos(ang), (1, LANES // A_DH))
    sin = jnp.tile(jnp.sin(ang) * sign[None, :], (1, LANES // A_DH))
    return cos, sin


def _proj_weights(w_in, b_gates, q_norm_w, k_norm_w):
    pad_g = jnp.zeros((D_MODEL, LANES - N_GATES), F32)
    wq = w_in[:, G_END:G_END + A_W].reshape(D_MODEL, A_HEADS, 1, A_DH)
    grp = (jnp.arange(A_HEADS)[:, None] // A_REP == jnp.arange(A_KV)[None, :]).astype(F32)
    wq_pad = (wq * grp[None, :, :, None]).reshape(D_MODEL, A_HEADS * LANES)
    w_all = jnp.concatenate([w_in[:, :4 * M_W], w_in[:, 4 * M_W:G_END], pad_g, wq_pad,
                             w_in[:, G_END + A_W:G_END + A_W + A_KVW], w_in[:, G_END + A_W + A_KVW:]], axis=1).astype(BF16)
    wgt = w_in[:, 4 * M_W:G_END].T.astype(BF16)
    bgr = jnp.concatenate([b_gates, jnp.zeros((LANES - N_GATES,), F32)]).reshape(1, LANES)
    bgc = b_gates.reshape(N_GATES, 1)
    nq = jnp.tile(q_norm_w, LANES // A_DH).reshape(1, LANES)
    nk = jnp.tile(k_norm_w, LANES // A_DH).reshape(1, LANES)
    return w_all, wgt, bgr, bgc, nq, nk


def _forward(x, c, ctx, c_ctx, w_mod, b_mod, w_in, conv_w, conv_b, b_gates, mh_norm_w, q_norm_w, k_norm_w,
             w_out, ln1_g, ln1_b, peer_wq, peer_keys, peer_u, peer_v, ln2_g, ln2_b):
    batch, seq_len, _ = x.shape
    ctx_len = ctx.shape[1]
    l = 0
    D = D_MODEL

    cc = jnp.zeros((SUBLANES, D), F32).at[:batch].set(c).at[batch].set(c_ctx)
    mod = _mod_call(cc, w_mod[l], b_mod[l])
    modl = mod[:batch].reshape(batch, N_MOD, D)
    modc = mod[batch:batch + 1].reshape(1, N_MOD, D)

    w_all, wgt, bgr, bgc, nq, nk = _proj_weights(w_in[l], b_gates[l], q_norm_w[l], k_norm_w[l])
    cw = conv_w[l]
    cb = conv_b[l].reshape(1, 2 * M_W)
    cos, sin = _rope_tables(seq_len)

    xf = x.reshape(batch * seq_len, D)
    cf = ctx.reshape(batch * ctx_len, D)
    proj = functools.partial(_proj_call, w_all=w_all, wgt=wgt, cw=cw, cb=cb, bgr=bgr, bgc=bgc, nq=nq, nk=nk)
    ctl = min(512, ctx_len)
    (qm_c, km_c, vm_c, _, g_c, gt_c, _, ka_c, va_c) = proj(
        cf, modc[:, 0:2], cos=cos[:ctl], sin=sin[:ctl], seq_len=ctx_len, rope=False)
    (qm, km, vm, om, g, gt, qa, ka, va) = proj(xf, modl[:, 0:2], cos=cos, sin=sin, seq_len=seq_len, rope=True)

    nch = 2 * M_HEADS
    ct0 = jnp.zeros((batch, nch, M_DH, 2 * M_DH), F32)
    m0 = jnp.zeros((batch, nch, LANES), F32)
    _, _, ct_c, m_c = _mlstm_call(qm_c, km_c, vm_c, g_c, gt_c, ct0, m0, batch=batch, seq_len=ctx_len)
    hf, hb, _, _ = _mlstm_call(qm, km, vm, g, gt, ct_c, m_c, batch=batch, seq_len=seq_len)

    k_all = jnp.concatenate([ka.reshape(batch, seq_len, LANES), ka_c.reshape(batch, ctx_len, LANES)], axis=1)
    v_all = jnp.concatenate([va.reshape(batch, seq_len, LANES), va_c.reshape(batch, ctx_len, LANES)], axis=1)
    att = _attn_call(qa, k_all, v_all, batch=batch, seq_len=seq_len)

    wo = w_out[l]
    wom = wo[:M_W].astype(BF16)
    woa = wo[M_W:].reshape(A_HEADS, 1, A_DH, D)
    grp = (jnp.arange(A_HEADS)[:, None] // A_REP == jnp.arange(A_KV)[None, :]).astype(F32)
    woa = (woa * grp[:, :, None, None]).reshape(A_HEADS, LANES, D).astype(BF16)
    mod_mix = jnp.stack([modl[:, 2], modl[:, 3], modl[:, 4]], axis=1)
    keys = peer_keys[l].reshape(2 * P_HEADS, N_KEYS, P_HALF).astype(BF16)
    x1, st = _mix_call(hf, hb, om, att, xf, mod_mix, wom, woa, mh_norm_w[l].reshape(1, M_W),
                       ln1_g[l].reshape(1, D), ln1_b[l].reshape(1, D), peer_wq[l].astype(BF16), keys, seq_len=seq_len)

    r2, e2, c1, e1 = _topk_call(st)
    mod_peer = jnp.stack([modl[:, 3], modl[:, 4], modl[:, 5]], axis=1)
    out = _peer_call(x1, mod_peer, peer_u[l].astype(BF16), peer_v[l].T.astype(BF16), r2, e2, c1, e1,
                     ln2_g[l].reshape(1, D), ln2_b[l].reshape(1, D), seq_len=seq_len)
    return out.reshape(batch, seq_len, D)


def kernel(x, c, ctx, c_ctx, w_mod, b_mod, w_in, conv_w, conv_b, b_gates, mh_norm_w, q_norm_w, k_norm_w, w_out,
           ln1_g, ln1_b, peer_wq, peer_keys, peer_u, peer_v, ln2_g, ln2_b):
    return _forward(x, c, ctx, c_ctx, w_mod, b_mod, w_in, conv_w, conv_b, b_gates, mh_norm_w, q_norm_w, k_norm_w,
                    w_out, ln1_g, ln1_b, peer_wq, peer_keys, peer_u, peer_v, ln2_g, ln2_b)
```

```python
import functools

import jax
import jax.numpy as jnp
from jax import lax
from jax.experimental import pallas as pl
from jax.experimental.pallas import tpu as pltpu

F32 = jnp.float32
BF16 = jnp.bfloat16

D_MODEL = 1024
DEPTH = 1
GRID_W = 64
M_HEADS = 4
M_DH = 128
M_W = M_HEADS * M_DH
M_CHUNK = 128
N_GATES = 4 * M_HEADS
A_HEADS = 8
A_KV = 2
A_REP = A_HEADS // A_KV
A_DH = 64
A_W = A_HEADS * A_DH
A_KVW = A_KV * A_DH
ROPE_THETA = 10000.0
ATTN_SCALE = A_DH ** -0.5
G_END = 4 * M_W + N_GATES
P_HEADS = 8
N_KEYS = 128
P_HALF = 128
P_QDIM = 2 * P_HALF
P_TOPK = 16
N_MOD = 6
DEEPNORM_ALPHA = (2.0 * DEPTH) ** 0.25
LN_EPS = 1e-5
RMS_EPS = 1e-6

LANES = 128
SUBLANES = 8
VMEM_LIMIT = 56 * 1024 * 1024

C_QK = 0
C_VM = 2 * M_W
C_OM = 3 * M_W
C_G = 4 * M_W
C_QA = C_G + LANES
C_KA = C_QA + A_HEADS * LANES
C_VA = C_KA + LANES
C_END = C_VA + LANES

NEG_INF = float("-inf")


def _cparams(sem):
    return pltpu.CompilerParams(dimension_semantics=sem, vmem_limit_bytes=VMEM_LIMIT)


def _dot(a, b):
    return jnp.dot(a, b, preferred_element_type=F32)


def _dot_nt(a, b):
    return lax.dot_general(a, b, (((1,), (1,)), ((), ())), preferred_element_type=F32)


def _dot_tn(a, b):
    return lax.dot_general(a, b, (((0,), (0,)), ((), ())), preferred_element_type=F32)


def _split3(x):
    hi = x.astype(BF16)
    r = x - hi.astype(F32)
    mid = r.astype(BF16)
    lo = (r - mid.astype(F32)).astype(BF16)
    return hi, mid, lo


def _pack_bf16(x):
    return pltpu.bitcast(x.astype(BF16), jnp.uint32)


def _unpack_bf16(x):
    return pltpu.bitcast(x, BF16)


def _sigmoid(x):
    return 1.0 / (1.0 + jnp.exp(-x))


def _log_sigmoid(x):
    return jnp.minimum(x, 0.0) - jnp.log(1.0 + jnp.exp(-jnp.abs(x)))


def _layer_norm(y, g, b):
    mu = jnp.mean(y, axis=-1, keepdims=True)
    yc = y - mu
    var = jnp.mean(yc * yc, axis=-1, keepdims=True)
    return yc * lax.rsqrt(var + LN_EPS) * g + b


def _mod_body(c_ref, w_ref, b_ref, o_ref):
    c = c_ref[...]
    a = c * _sigmoid(c)
    hi, mid, lo = _split3(a)
    w = w_ref[...]
    whi = w.astype(BF16)
    wlo = (w - whi.astype(F32)).astype(BF16)
    o_ref[...] = (_dot(hi, whi) + _dot(mid, whi) + _dot(hi, wlo) + _dot(lo, whi) + _dot(mid, wlo)) + b_ref[...]


def _mod_call(cc, w, b):
    n = w.shape[1]
    tn = 1536
    return pl.pallas_call(
        _mod_body,
        grid=(n // tn,),
        in_specs=[pl.BlockSpec((SUBLANES, D_MODEL), lambda j: (0, 0)),
                  pl.BlockSpec((D_MODEL, tn), lambda j: (0, j)),
                  pl.BlockSpec((1, tn), lambda j: (0, j))],
        out_specs=pl.BlockSpec((SUBLANES, tn), lambda j: (0, j)),
        out_shape=jax.ShapeDtypeStruct((SUBLANES, n), F32),
        compiler_params=_cparams(("arbitrary",)),
        name="mod",
    )(cc, w, b.reshape(1, n))


def _proj_body(x_ref, xp_ref, xn_ref, mod_ref, w_ref, wgt_ref, cw_ref, cb_ref, bgr_ref, bgc_ref,
               nq_ref, nk_ref, cos_ref, sin_ref,
               qm_ref, km_ref, vm_ref, om_ref, g_ref, gt_ref, qa_ref, ka_ref, va_ref,
               *, tl, tiles_per_seq, rope):
    t = lax.rem(pl.program_id(0), tiles_per_seq)
    shift = mod_ref[0, 0:1, :]
    scale1 = 1.0 + mod_ref[0, 1:2, :]

    def modulate(v):
        return (v * scale1 + shift).astype(BF16)

    h = modulate(x_ref[...])
    z = _dot(h, w_ref[...])
    wqk = w_ref[:, C_QK:C_VM]
    zp = _dot(modulate(xp_ref[...]), wqk)[SUBLANES - 1:SUBLANES, :]
    zn = _dot(modulate(xn_ref[...]), wqk)[0:1, :]
    zp = jnp.where(t != 0, zp, 0.0)
    zn = jnp.where(t != tiles_per_seq - 1, zn, 0.0)
    zqk = z[:, C_QK:C_VM]
    rows = lax.broadcasted_iota(jnp.int32, (tl, 1), 0)
    z_dn = jnp.where(rows == 0, zp, pltpu.roll(zqk, 1, 0))
    z_up = jnp.where(rows == tl - 1, zn, pltpu.roll(zqk, tl - 1, 0))
    y = cw_ref[0:1, :] * z_dn + cw_ref[1:2, :] * zqk + cw_ref[2:3, :] * z_up + cb_ref[...]
    y = y * _sigmoid(y)
    qm_ref[...] = (y[:, :M_W] * (M_DH ** -0.5)).astype(BF16)
    km_ref[...] = y[:, M_W:].astype(BF16)
    vm_ref[...] = z[:, C_VM:C_OM].astype(BF16)
    om_ref[...] = z[:, C_OM:C_G]
    g_ref[...] = z[:, C_G:C_QA] + bgr_ref[...]
    gt_ref[...] = _dot_nt(wgt_ref[...], h) + bgc_ref[...]

    lane = lax.broadcasted_iota(jnp.int32, (1, LANES), 1)
    if rope:
        cos = cos_ref[...]
        sin = sin_ref[...]
        lo = lax.rem(lane, 32) < 16

    def rope_fn(v):
        if not rope:
            return v
        partner = jnp.where(lo, pltpu.roll(v, LANES - 16, 1), pltpu.roll(v, 16, 1))
        return v * cos + partner * sin

    for hh in range(A_HEADS):
        zq = z[:, C_QA + LANES * hh:C_QA + LANES * (hh + 1)]
        ms = jnp.sum(zq * zq, axis=-1, keepdims=True) * (1.0 / A_DH)
        qn = zq * lax.rsqrt(ms + RMS_EPS) * nq_ref[...]
        qa_ref[hh] = (rope_fn(qn) * ATTN_SCALE).astype(BF16)
    zk = z[:, C_KA:C_VA]
    sq = zk * zk
    first = lane < A_DH
    s0 = jnp.sum(jnp.where(first, sq, 0.0), axis=-1, keepdims=True)
    s1 = jnp.sum(jnp.where(first, 0.0, sq), axis=-1, keepdims=True)
    ms = jnp.where(first, s0, s1) * (1.0 / A_DH)
    kn = zk * lax.rsqrt(ms + RMS_EPS) * nk_ref[...]
    ka_ref[...] = rope_fn(kn).astype(BF16)
    va_ref[...] = z[:, C_VA:C_END].astype(BF16)


def _proj_call(xf, mod2, w_all, wgt, cw, cb, bgr, bgc, nq, nk, cos, sin, *, seq_len, rope):
    n = xf.shape[0]
    tl = min(512, seq_len)
    tps = seq_len // tl
    nt = n // tl
    hb = tl // SUBLANES
    nblk8 = n // SUBLANES
    const = lambda i: (0, 0)
    body = functools.partial(_proj_body, tl=tl, tiles_per_seq=tps, rope=rope)
    out_shapes = (
        jax.ShapeDtypeStruct((n, M_W), BF16),
        jax.ShapeDtypeStruct((n, M_W), BF16),
        jax.ShapeDtypeStruct((n, M_W), BF16),
        jax.ShapeDtypeStruct((n, M_W), F32),
        jax.ShapeDtypeStruct((n, LANES), F32),
        jax.ShapeDtypeStruct((N_GATES, n), F32),
        jax.ShapeDtypeStruct((A_HEADS, n, LANES), BF16),
        jax.ShapeDtypeStruct((n, LANES), BF16),
        jax.ShapeDtypeStruct((n, LANES), BF16),
    )
    row = lambda i: (i, 0)
    out_specs = (
        pl.BlockSpec((tl, M_W), row), pl.BlockSpec((tl, M_W), row), pl.BlockSpec((tl, M_W), row),
        pl.BlockSpec((tl, M_W), row), pl.BlockSpec((tl, LANES), row),
        pl.BlockSpec((N_GATES, tl), lambda i: (0, i)),
        pl.BlockSpec((A_HEADS, tl, LANES), lambda i: (0, i, 0)),
        pl.BlockSpec((tl, LANES), row), pl.BlockSpec((tl, LANES), row),
    )
    in_specs = [
        pl.BlockSpec((tl, D_MODEL), row),
        pl.BlockSpec((SUBLANES, D_MODEL), lambda i: (jnp.maximum(i * hb - 1, 0), 0)),
        pl.BlockSpec((SUBLANES, D_MODEL), lambda i: (jnp.minimum((i + 1) * hb, nblk8 - 1), 0)),
        pl.BlockSpec((1, 2, D_MODEL), lambda i: (i // tps if mod2.shape[0] > 1 else 0, 0, 0)),
        pl.BlockSpec((D_MODEL, C_END), const),
        pl.BlockSpec((N_GATES, D_MODEL), const),
        pl.BlockSpec((3, 2 * M_W), const),
        pl.BlockSpec((1, 2 * M_W), const),
        pl.BlockSpec((1, LANES), const),
        pl.BlockSpec((N_GATES, 1), const),
        pl.BlockSpec((1, LANES), const),
        pl.BlockSpec((1, LANES), const),
        pl.BlockSpec((tl, LANES), lambda i: (lax.rem(i, tps), 0)),
        pl.BlockSpec((tl, LANES), lambda i: (lax.rem(i, tps), 0)),
    ]
    return pl.pallas_call(
        body, grid=(nt,), in_specs=in_specs, out_specs=out_specs, out_shape=out_shapes,
        compiler_params=_cparams(("arbitrary",)), name="proj_rope" if rope else "proj_ctx",
    )(xf, xf, xf, mod2, w_all, wgt, cw, cb, bgr, bgc, nq, nk, cos, sin)


def _mlstm_body(qf_ref, kf_ref, vf_ref, gf_ref, gtf_ref, qb_ref, kb_ref, vb_ref, gb_ref, gtb_ref,
                ct0_ref, m0_ref, hf_ref, hb_ref, cto_ref, mo_ref, ct_s, m_s, *, sub):
    c = pl.program_id(1)
    nc = pl.num_programs(1)
    T = M_CHUNK

    @pl.when(c == 0)
    def _():
        ct_s[...] = ct0_ref[0]
        m_s[...] = m0_ref[0]

    ri = lax.broadcasted_iota(jnp.int32, (T, T), 0)
    ci = lax.broadcasted_iota(jnp.int32, (T, T), 1)
    le = ri <= ci
    ge = ri >= ci
    u_le = jnp.where(le, 1.0, 0.0).astype(BF16)
    u_ge = jnp.where(ge, 1.0, 0.0).astype(BF16)
    ones_col = jnp.where(ci == 0, 1.0, 0.0).astype(BF16)

    dirs = ((qf_ref, kf_ref, vf_ref, gf_ref, gtf_ref, hf_ref, ge, u_le, u_ge),
            (qb_ref, kb_ref, vb_ref, gb_ref, gtb_ref, hb_ref, le, u_ge, u_le))
    for d, (q_ref, k_ref, v_ref, g_ref, gt_ref, h_ref, mask, u_row, u_col) in enumerate(dirs):
        ct = [ct_s[M_HEADS * d + h] for h in range(M_HEADS)]
        m_run = [m_s[M_HEADS * d + h:M_HEADS * d + h + 1, 0:1] for h in range(M_HEADS)]
        for sc in (range(sub) if d == 0 else range(sub - 1, -1, -1)):
            rows = slice(T * sc, T * (sc + 1))
            grp = gt_ref[2 * M_HEADS * d:2 * M_HEADS * (d + 1), rows]
            lf8 = _log_sigmoid(grp)
            a, b_, c_ = _split3(lf8)
            b8 = _dot(a, u_row) + _dot(b_, u_row) + _dot(c_, u_row)
            gcol = g_ref[rows, :]
            a, b_, c_ = _split3(_log_sigmoid(gcol))
            bcol = _dot(u_col, a) + _dot(u_col, b_) + _dot(u_col, c_)
            for h in range(M_HEADS):
                g_row = grp[h:h + 1, :] - b8[M_HEADS + h:M_HEADS + h + 1, :]
                li_col = gcol[:, 2 * M_HEADS * d + h:2 * M_HEADS * d + h + 1]
                b_col = bcol[:, 2 * M_HEADS * d + M_HEADS + h:2 * M_HEADS * d + M_HEADS + h + 1]
                g_col = li_col - b_col
                m_prev = m_run[h]
                d0 = jnp.where(mask, g_row, NEG_INF)
                mx = jnp.maximum(m_prev, jnp.max(d0, axis=1, keepdims=True))
                dmat = jnp.exp(d0 - mx)
                sl = slice(M_DH * h, M_DH * (h + 1))
                q = q_ref[rows, sl]
                k = k_ref[rows, sl]
                v = v_ref[rows, sl]
                s = _dot_nt(q, k) * dmat
                vaug = jnp.concatenate([v, ones_col], axis=1)
                num = _dot(s.astype(BF16), vaug) + jnp.exp(m_prev - mx) * _dot(q, ct[h].astype(BF16))
                den = num[:, M_DH:M_DH + 1]
                floor = jnp.exp(-(b_col + mx))
                h_ref[rows, sl] = num[:, :M_DH] / jnp.maximum(jnp.abs(den), floor)
                mxl = jnp.maximum(m_prev, jnp.max(g_row, axis=1, keepdims=True))
                b_last = jnp.sum(lf8[M_HEADS + h:M_HEADS + h + 1, :], axis=1, keepdims=True)
                w_col = jnp.exp(g_col - mxl)
                wv = (w_col * vaug.astype(F32)).astype(BF16)
                ct[h] = jnp.exp(m_prev - mxl) * ct[h] + _dot_tn(k, wv)
                m_run[h] = b_last + mxl
        for h in range(M_HEADS):
            ch = M_HEADS * d + h
            ct_s[ch] = ct[h]
            m_s[ch:ch + 1, :] = jnp.broadcast_to(m_run[h], (1, LANES))

    @pl.when(c == nc - 1)
    def _():
        cto_ref[0] = ct_s[...]
        mo_ref[0] = m_s[...]


def _mlstm_call(qm, km, vm, g, gt, ct0, m0, *, batch, seq_len):
    n = qm.shape[0]
    sub = min(4, seq_len // M_CHUNK)
    rows = sub * M_CHUNK
    nc = seq_len // rows
    nch = 2 * M_HEADS
    fwd = lambda b, c: (b * nc + c, 0)
    bwd = lambda b, c: (b * nc + nc - 1 - c, 0)
    fwd_t = lambda b, c: (0, b * nc + c)
    bwd_t = lambda b, c: (0, b * nc + nc - 1 - c)
    tile = lambda im: pl.BlockSpec((rows, M_W), im)
    in_specs = [tile(fwd), tile(fwd), tile(fwd), pl.BlockSpec((rows, LANES), fwd), pl.BlockSpec((N_GATES, rows), fwd_t),
                tile(bwd), tile(bwd), tile(bwd), pl.BlockSpec((rows, LANES), bwd), pl.BlockSpec((N_GATES, rows), bwd_t),
                pl.BlockSpec((1, nch, M_DH, 2 * M_DH), lambda b, c: (b, 0, 0, 0)),
                pl.BlockSpec((1, nch, LANES), lambda b, c: (b, 0, 0))]
    out_specs = (tile(fwd), tile(bwd),
                 pl.BlockSpec((1, nch, M_DH, 2 * M_DH), lambda b, c: (b, 0, 0, 0)),
                 pl.BlockSpec((1, nch, LANES), lambda b, c: (b, 0, 0)))
    out_shape = (jax.ShapeDtypeStruct((n, M_W), F32), jax.ShapeDtypeStruct((n, M_W), F32),
                 jax.ShapeDtypeStruct((batch, nch, M_DH, 2 * M_DH), F32),
                 jax.ShapeDtypeStruct((batch, nch, LANES), F32))
    return pl.pallas_call(
        functools.partial(_mlstm_body, sub=sub), grid=(batch, nc), in_specs=in_specs, out_specs=out_specs,
        out_shape=out_shape,
        scratch_shapes=[pltpu.VMEM((nch, M_DH, 2 * M_DH), F32), pltpu.VMEM((nch, LANES), F32)],
        compiler_params=_cparams(("arbitrary", "arbitrary")), name="mlstm",
    )(qm, km, vm, g, gt, qm, km, vm, g, gt, ct0, m0)


def _attn_body(q_ref, k_ref, v_ref, o_ref, *, tk, n_keys):
    ci = lax.broadcasted_iota(jnp.int32, (tk, LANES), 1)
    ones_col = jnp.where(ci == 0, 1.0, 0.0).astype(BF16)
    m = [None] * A_HEADS
    acc = [None] * A_HEADS
    for j in range(n_keys // tk):
        kt = k_ref[0, tk * j:tk * (j + 1), :]
        vaug = jnp.concatenate([v_ref[0, tk * j:tk * (j + 1), :], ones_col], axis=1)
        for h in range(A_HEADS):
            s = _dot_nt(q_ref[h], kt)
            row_max = jnp.max(s, axis=1, keepdims=True)
            if j == 0:
                m_new = row_max
                acc[h] = _dot(jnp.exp(s - m_new).astype(BF16), vaug)
            else:
                m_new = jnp.maximum(m[h], row_max)
                acc[h] = jnp.exp(m[h] - m_new) * acc[h] + _dot(jnp.exp(s - m_new).astype(BF16), vaug)
            m[h] = m_new
    for h in range(A_HEADS):
        o_ref[h] = (acc[h][:, :LANES] / acc[h][:, LANES:LANES + 1]).astype(BF16)


def _attn_call(qa, k_all, v_all, *, batch, seq_len):
    n = qa.shape[1]
    n_keys = k_all.shape[1]
    tq = 256
    tk = next(t for t in (2816, 1408, 768, 256) if n_keys % t == 0)
    nq = seq_len // tq
    body = functools.partial(_attn_body, tk=tk, n_keys=n_keys)
    return pl.pallas_call(
        body, grid=(batch, nq),
        in_specs=[pl.BlockSpec((A_HEADS, tq, LANES), lambda b, i: (0, b * nq + i, 0)),
                  pl.BlockSpec((1, n_keys, LANES), lambda b, i: (b, 0, 0)),
                  pl.BlockSpec((1, n_keys, LANES), lambda b, i: (b, 0, 0))],
        out_specs=pl.BlockSpec((A_HEADS, tq, LANES), lambda b, i: (0, b * nq + i, 0)),
        out_shape=jax.ShapeDtypeStruct((A_HEADS, n, LANES), BF16),
        compiler_params=_cparams(("arbitrary", "arbitrary")), name="attn",
    )(qa, k_all, v_all)


def _mix_body(hf_ref, hb_ref, om_ref, att_ref, x_ref, mod_ref, wom_ref, woa_ref, mhw_ref, lg_ref, lb_ref,
              wq_ref, keys_ref, x1_ref, st_ref):
    hsum = hf_ref[...] + hb_ref[...]
    om = om_ref[...]
    parts = []
    for h in range(M_HEADS):
        sl = slice(M_DH * h, M_DH * (h + 1))
        blk = hsum[:, sl]
        mu = jnp.mean(blk, axis=-1, keepdims=True)
        xc = blk - mu
        var = jnp.mean(xc * xc, axis=-1, keepdims=True)
        hn = xc * lax.rsqrt(var + LN_EPS) * mhw_ref[:, sl]
        parts.append((_sigmoid(om[:, sl]) * hn).astype(BF16))
    mix = _dot(jnp.concatenate(parts, axis=1), wom_ref[...])
    for h in range(A_HEADS):
        mix = mix + _dot(att_ref[h], woa_ref[h])
    g1 = mod_ref[0, 0:1, :]
    sh2 = mod_ref[0, 1:2, :]
    sc2 = mod_ref[0, 2:3, :]
    x1 = _layer_norm(DEEPNORM_ALPHA * x_ref[...] + g1 * mix, lg_ref[...], lb_ref[...])
    x1_ref[...] = x1
    hp = (x1 * (1.0 + sc2) + sh2).astype(BF16)
    qp = _dot(hp, wq_ref[...])
    for j in range(2 * P_HEADS):
        blk = qp[:, P_HALF * j:P_HALF * (j + 1)].astype(BF16)
        st_ref[j] = _dot_nt(keys_ref[j], blk)


def _mix_call(hf, hb, om, att, xf, mod3, wom, woa, mhw, lg, lb, wq, keys, *, seq_len):
    n = xf.shape[0]
    tl = 256
    tps = seq_len // tl
    row = lambda i: (i, 0)
    const2 = lambda i: (0, 0)
    const3 = lambda i: (0, 0, 0)
    return pl.pallas_call(
        _mix_body, grid=(n // tl,),
        in_specs=[pl.BlockSpec((tl, M_W), row), pl.BlockSpec((tl, M_W), row), pl.BlockSpec((tl, M_W), row),
                  pl.BlockSpec((A_HEADS, tl, LANES), lambda i: (0, i, 0)),
                  pl.BlockSpec((tl, D_MODEL), row),
                  pl.BlockSpec((1, 3, D_MODEL), lambda i: (i // tps, 0, 0)),
                  pl.BlockSpec((M_W, D_MODEL), const2),
                  pl.BlockSpec((A_HEADS, LANES, D_MODEL), const3),
                  pl.BlockSpec((1, M_W), const2), pl.BlockSpec((1, D_MODEL), const2), pl.BlockSpec((1, D_MODEL), const2),
                  pl.BlockSpec((D_MODEL, P_HEADS * P_QDIM), const2),
                  pl.BlockSpec((2 * P_HEADS, N_KEYS, P_HALF), const3)],
        out_specs=(pl.BlockSpec((tl, D_MODEL), row),
                   pl.BlockSpec((2 * P_HEADS, N_KEYS, tl), lambda i: (0, 0, i))),
        out_shape=(jax.ShapeDtypeStruct((n, D_MODEL), F32),
                   jax.ShapeDtypeStruct((2 * P_HEADS, N_KEYS, n), F32)),
        compiler_params=_cparams(("arbitrary",)), name="mix",
    )(hf, hb, om, att, xf, mod3, wom, woa, mhw, lg, lb, wq, keys)


def _sublane_all(op, x):
    x = op(x, pltpu.roll(x, 4, 0))
    x = op(x, pltpu.roll(x, 2, 0))
    return op(x, pltpu.roll(x, 1, 0))


def _tree(op, xs):
    xs = list(xs)
    while len(xs) > 1:
        xs = [op(xs[i], xs[i + 1]) if i + 1 < len(xs) else xs[i] for i in range(0, len(xs), 2)]
    return xs[0]


def _extract16(blocks, n_ranked):
    v = list(blocks)
    rank = [None] * len(v)
    vals = []
    for i in range(P_TOPK):
        m = _sublane_all(jnp.maximum, _tree(jnp.maximum, v))
        vals.append(m)
        for kb in range(len(v)):
            hit = v[kb] == m
            if i < n_ranked:
                rank[kb] = jnp.where(hit, float(i), float(N_KEYS) if rank[kb] is None else rank[kb])
            if i + 1 < P_TOPK:
                v[kb] = jnp.where(hit, NEG_INF, v[kb])
    return vals, rank


_LOW_RANKS = 5


def _topk_unit(st_ref, r2_ref, e2_ref, c1_ref, e1_ref, h, lanes):
    nkb = N_KEYS // SUBLANES
    s1 = [st_ref[2 * h, SUBLANES * kb:SUBLANES * (kb + 1), lanes] for kb in range(nkb)]
    s2 = [st_ref[2 * h + 1, SUBLANES * kb:SUBLANES * (kb + 1), lanes] for kb in range(nkb)]
    v1, rank1 = _extract16(s1, _LOW_RANKS)
    v2, rank2 = _extract16(s2, P_TOPK)
    row = lax.broadcasted_iota(jnp.int32, v1[0].shape, 0)

    def column(vals, base):
        out = vals[base + SUBLANES - 1]
        for r in range(SUBLANES - 2, -1, -1):
            out = jnp.where(row == r, vals[base + r], out)
        return out

    s2lo = column(v2, 0)
    s2hi = column(v2, SUBLANES)
    s1hi = column(v1, 6)
    cands = [
        v1[0] + s2lo, v1[0] + s2hi, v1[1] + s2lo,
        jnp.where(row < 5, v1[2] + s2lo, v1[4] + pltpu.roll(s2lo, 5, 0)),
        jnp.where(row < 4, v1[3] + s2lo,
                  jnp.where(row < 6, v1[5] + pltpu.roll(s2lo, 4, 0), v1[6] + pltpu.roll(s2lo, 6, 0))),
        jnp.where(row < 2, v1[7] + s2lo, s1hi + v2[0]),
        jnp.where(row < 2, jnp.where(row == 0, v1[14], v1[15]) + v2[0], NEG_INF),
    ]
    work = list(cands)
    for it in range(P_TOPK):
        tau = _sublane_all(jnp.maximum, _tree(jnp.maximum, work))
        if it + 1 < P_TOPK:
            work = [jnp.where(w == tau, NEG_INF, w) for w in work]
    top = v1[0] + v2[0]
    sel = [c >= tau for c in cands]
    ind = [jnp.where(m, 1.0, 0.0) for m in sel]
    z = _sublane_all(jnp.add, _tree(jnp.add, [jnp.where(m, jnp.exp(c - top), 0.0) for m, c in zip(sel, cands)]))
    cnt = [None] * _LOW_RANKS
    cnt[0] = _sublane_all(jnp.add, ind[0] + ind[1])
    cnt[1] = _sublane_all(jnp.add, ind[2])
    cnt[2] = _sublane_all(jnp.add, jnp.where(row < 5, ind[3], 0.0))
    cnt[4] = _sublane_all(jnp.add, ind[3]) - cnt[2]
    cnt[3] = _sublane_all(jnp.add, jnp.where(row < 4, ind[4], 0.0))
    zinv = 1.0 / z
    for kb in range(nkb):
        rows = slice(SUBLANES * kb, SUBLANES * (kb + 1))
        c1 = jnp.where(s1[kb] + v2[0] >= tau, 1.0, 0.0) + jnp.where(s1[kb] + v2[1] >= tau, 1.0, 0.0)
        for i in range(_LOW_RANKS):
            c1 = jnp.where(rank1[kb] == float(i), cnt[i], c1)
        c1_ref[h, rows, lanes] = c1
        e1_ref[h, rows, lanes] = jnp.exp(s1[kb] - v1[0]) * zinv
    for pb in range(nkb // 2):
        rows = slice(SUBLANES * pb, SUBLANES * (pb + 1))
        r2 = jnp.concatenate([rank2[2 * pb], rank2[2 * pb + 1]], axis=0).astype(BF16)
        e2 = jnp.concatenate([jnp.exp(s2[2 * pb] - v2[0]), jnp.exp(s2[2 * pb + 1] - v2[0])], axis=0).astype(BF16)
        r2_ref[h, rows, lanes] = pltpu.bitcast(r2, jnp.uint32)
        e2_ref[h, rows, lanes] = pltpu.bitcast(e2, jnp.uint32)


def _topk_body(st_ref, r2_ref, e2_ref, c1_ref, e1_ref):
    for h in range(P_HEADS):
        for lt in range(st_ref.shape[2] // LANES):
            _topk_unit(st_ref, r2_ref, e2_ref, c1_ref, e1_ref, h, slice(LANES * lt, LANES * (lt + 1)))


def _topk_call(st):
    n = st.shape[2]
    tt = 256
    spec_in = pl.BlockSpec((2 * P_HEADS, N_KEYS, tt), lambda i: (0, 0, i))
    spec_out = pl.BlockSpec((P_HEADS, N_KEYS, tt), lambda i: (0, 0, i))
    shp = jax.ShapeDtypeStruct((P_HEADS, N_KEYS, n), F32)
    shp16 = jax.ShapeDtypeStruct((P_HEADS, N_KEYS // 2, n), jnp.uint32)
    spec16 = pl.BlockSpec((P_HEADS, N_KEYS // 2, tt), lambda i: (0, 0, i))
    return pl.pallas_call(
        _topk_body, grid=(n // tt,), in_specs=[spec_in], out_specs=(spec16, spec16, spec_out, spec_out),
        out_shape=(shp16, shp16, shp, shp),
        compiler_params=_cparams(("arbitrary",)), name="topk",
    )(st)


def _pack_body(x_ref, o_ref, *, transpose):
    x = x_ref[...]
    o_ref[...] = _pack_bf16(x.T if transpose else x)


def _pack_call(x, *, transpose):
    r, c = x.shape
    tr = 512
    if transpose:
        out_shape = jax.ShapeDtypeStruct((c // 2, r), jnp.uint32)
        out_spec = pl.BlockSpec((c // 2, tr), lambda i: (0, i))
    else:
        out_shape = jax.ShapeDtypeStruct((r // 2, c), jnp.uint32)
        out_spec = pl.BlockSpec((tr // 2, c), lambda i: (i, 0))
    return pl.pallas_call(
        functools.partial(_pack_body, transpose=transpose), grid=(r // tr,),
        in_specs=[pl.BlockSpec((tr, c), lambda i: (i, 0))], out_specs=out_spec, out_shape=out_shape,
        compiler_params=_cparams(("arbitrary",)), name="pack_t" if transpose else "pack",
    )(x)


PEER_CHUNK = 2 * N_KEYS
BF16_ROWS = 16


def _peer_gate_stage(at_ref, h_ref, r2_ref, e2_ref, c1_ref, e1_ref, a0, tt):
    for k in range(PEER_CHUNK // N_KEYS):
        for j in range(tt // LANES):
            cols = slice(LANES * j, LANES * (j + 1))
            c1b = [jnp.broadcast_to(c1_ref[h, a0 + k:a0 + k + 1, cols], (BF16_ROWS, LANES)).astype(BF16)
                   for h in range(P_HEADS)]
            e1b = [jnp.broadcast_to(e1_ref[h, a0 + k:a0 + k + 1, cols], (BF16_ROWS, LANES)).astype(BF16)
                   for h in range(P_HEADS)]
            for rb in range(N_KEYS // BF16_ROWS):
                words = slice(SUBLANES * rb, SUBLANES * (rb + 1))
                rows = slice(N_KEYS * k + BF16_ROWS * rb, N_KEYS * k + BF16_ROWS * (rb + 1))
                hrows = slice((N_KEYS * k) // 2 + SUBLANES * rb, (N_KEYS * k) // 2 + SUBLANES * (rb + 1))
                w = None
                for h in range(P_HEADS):
                    e2v = pltpu.bitcast(e2_ref[h, words, cols], BF16)
                    r2v = pltpu.bitcast(r2_ref[h, words, cols], BF16)
                    term = jnp.where(r2v < c1b[h], e2v, jnp.zeros_like(e2v)) * e1b[h]
                    w = term if w is None else w + term
                a = at_ref[rows, cols]
                act = 0.5 * a * (1.0 + lax.erf(a * (2.0 ** -0.5)))
                h_ref[hrows, cols] = pltpu.bitcast(act.astype(BF16) * w, jnp.uint32)


def _peer_body(x1_ref, mod_ref, u_ref, vt_ref, r2_ref, e2_ref, c1_ref, e1_ref, lg_ref, lb_ref,
               o_ref, hpt_s, acc_s, at0_s, at1_s, h0_s, h1_s, *, et, tt):
    e = pl.program_id(1)

    @pl.when(e == 0)
    def _():
        sh2 = mod_ref[0, 0:1, :]
        sc2 = mod_ref[0, 1:2, :]
        hpt_s[...] = _pack_bf16((x1_ref[...] * (1.0 + sc2) + sh2).T)
        acc_s[...] = jnp.zeros(acc_s.shape, F32)

    at_bufs = (at0_s, at1_s)
    h_bufs = (h0_s, h1_s)
    nch = et // PEER_CHUNK

    def rows(c):
        return slice(PEER_CHUNK * c, PEER_CHUNK * (c + 1))

    def prow(c):
        return slice(PEER_CHUNK // 2 * c, PEER_CHUNK // 2 * (c + 1))

    def stage_a(c):
        return _dot(_unpack_bf16(u_ref[prow(c), :]), _unpack_bf16(hpt_s[...]))

    at_bufs[0][...] = stage_a(0)
    for c in range(nch):
        if c + 1 < nch:
            at_bufs[(c + 1) % 2][...] = stage_a(c + 1)
        _peer_gate_stage(at_bufs[c % 2], h_bufs[c % 2], r2_ref, e2_ref, c1_ref, e1_ref,
                         c * (PEER_CHUNK // N_KEYS), tt)
        acc_s[...] += _dot(_unpack_bf16(vt_ref[:, rows(c)]), _unpack_bf16(h_bufs[c % 2][...]))

    @pl.when(e == pl.num_programs(1) - 1)
    def _():
        g2 = mod_ref[0, 2:3, :]
        y = DEEPNORM_ALPHA * x1_ref[...] + g2 * acc_s[...].T
        o_ref[...] = _layer_norm(y, lg_ref[...], lb_ref[...])


def _peer_call(x1, mod3, u, v, r2, e2, c1, e1, lg, lb, *, seq_len):
    n = x1.shape[0]
    n_exp = 2 * u.shape[0]
    tt = min(1024, seq_len)
    et = 1024
    tps = seq_len // tt
    na = et // N_KEYS
    body = functools.partial(_peer_body, et=et, tt=tt)
    tok = lambda t, e: (t, 0)
    const2 = lambda t, e: (0, 0)
    full = pl.BlockSpec((P_HEADS, N_KEYS // 2, tt), lambda t, e: (0, 0, t))
    part = pl.BlockSpec((P_HEADS, na, tt), lambda t, e: (0, e, t))
    return pl.pallas_call(
        body, grid=(n // tt, n_exp // et),
        in_specs=[pl.BlockSpec((tt, D_MODEL), tok),
                  pl.BlockSpec((1, 3, D_MODEL), lambda t, e: (t // tps, 0, 0)),
                  pl.BlockSpec((et // 2, D_MODEL), lambda t, e: (e, 0)),
                  pl.BlockSpec((D_MODEL // 2, et), lambda t, e: (0, e)),
                  full, full, part, part,
                  pl.BlockSpec((1, D_MODEL), const2), pl.BlockSpec((1, D_MODEL), const2)],
        out_specs=pl.BlockSpec((tt, D_MODEL), tok),
        out_shape=jax.ShapeDtypeStruct((n, D_MODEL), F32),
        scratch_shapes=[pltpu.VMEM((D_MODEL // 2, tt), jnp.uint32), pltpu.VMEM((D_MODEL, tt), F32),
                        pltpu.VMEM((PEER_CHUNK, tt), F32), pltpu.VMEM((PEER_CHUNK, tt), F32),
                        pltpu.VMEM((PEER_CHUNK // 2, tt), jnp.uint32), pltpu.VMEM((PEER_CHUNK // 2, tt), jnp.uint32)],
        compiler_params=_cparams(("arbitrary", "arbitrary")), name="peer",
    )(x1, mod3, u, v, r2, e2, c1, e1, lg, lb)


def _rope_tables(n_tok):
    pos = jnp.arange(n_tok, dtype=jnp.int32)
    row = (pos // GRID_W).astype(F32)
    col = (pos % GRID_W).astype(F32)
    d = jnp.arange(A_DH)
    freq = ROPE_THETA ** (-(d % 16).astype(F32) / 16.0)
    p = jnp.where(d[None, :] < A_DH // 2, row[:, None], col[:, None])
    ang = p * freq[None, :]
    sign = jnp.where((d % 32) < 16, -1.0, 1.0)
    cos = jnp.tile(jnp.cos(ang), (1, LANES // A_DH))
    sin = jnp.tile(jnp.sin(ang) * sign[None, :], (1, LANES // A_DH))
    return cos, sin


def _proj_weights(w_in, b_gates, q_norm_w, k_norm_w):
    pad_g = jnp.zeros((D_MODEL, LANES - N_GATES), F32)
    wq = w_in[:, G_END:G_END + A_W].reshape(D_MODEL, A_HEADS, 1, A_DH)
    grp = (jnp.arange(A_HEADS)[:, None] // A_REP == jnp.arange(A_KV)[None, :]).astype(F32)
    wq_pad = (wq * grp[None, :, :, None]).reshape(D_MODEL, A_HEADS * LANES)
    w_all = jnp.concatenate([w_in[:, :4 * M_W], w_in[:, 4 * M_W:G_END], pad_g, wq_pad,
                             w_in[:, G_END + A_W:G_END + A_W + A_KVW], w_in[:, G_END + A_W + A_KVW:]], axis=1).astype(BF16)
    wgt = w_in[:, 4 * M_W:G_END].T.astype(BF16)
    bgr = jnp.concatenate([b_gates, jnp.zeros((LANES - N_GATES,), F32)]).reshape(1, LANES)
    bgc = b_gates.reshape(N_GATES, 1)
    nq = jnp.tile(q_norm_w, LANES // A_DH).reshape(1, LANES)
    nk = jnp.tile(k_norm_w, LANES // A_DH).reshape(1, LANES)
    return w_all, wgt, bgr, bgc, nq, nk


def _forward(x, c, ctx, c_ctx, w_mod, b_mod, w_in, conv_w, conv_b, b_gates, mh_norm_w, q_norm_w, k_norm_w,
             w_out, ln1_g, ln1_b, peer_wq, peer_keys, peer_u, peer_v, ln2_g, ln2_b):
    batch, seq_len, _ = x.shape
    ctx_len = ctx.shape[1]
    l = 0
    D = D_MODEL

    cc = jnp.zeros((SUBLANES, D), F32).at[:batch].set(c).at[batch].set(c_ctx)
    mod = _mod_call(cc, w_mod[l], b_mod[l])
    modl = mod[:batch].reshape(batch, N_MOD, D)
    modc = mod[batch:batch + 1].reshape(1, N_MOD, D)

    w_all, wgt, bgr, bgc, nq, nk = _proj_weights(w_in[l], b_gates[l], q_norm_w[l], k_norm_w[l])
    cw = conv_w[l]
    cb = conv_b[l].reshape(1, 2 * M_W)
    cos, sin = _rope_tables(seq_len)

    xf = x.reshape(batch * seq_len, D)
    cf = ctx.reshape(batch * ctx_len, D)
    proj = functools.partial(_proj_call, w_all=w_all, wgt=wgt, cw=cw, cb=cb, bgr=bgr, bgc=bgc, nq=nq, nk=nk)
    ctl = min(512, ctx_len)
    (qm_c, km_c, vm_c, _, g_c, gt_c, _, ka_c, va_c) = proj(
        cf, modc[:, 0:2], cos=cos[:ctl], sin=sin[:ctl], seq_len=ctx_len, rope=False)
    (qm, km, vm, om, g, gt, qa, ka, va) = proj(xf, modl[:, 0:2], cos=cos, sin=sin, seq_len=seq_len, rope=True)

    nch = 2 * M_HEADS
    ct0 = jnp.zeros((batch, nch, M_DH, 2 * M_DH), F32)
    m0 = jnp.zeros((batch, nch, LANES), F32)
    _, _, ct_c, m_c = _mlstm_call(qm_c, km_c, vm_c, g_c, gt_c, ct0, m0, batch=batch, seq_len=ctx_len)
    hf, hb, _, _ = _mlstm_call(qm, km, vm, g, gt, ct_c, m_c, batch=batch, seq_len=seq_len)

    k_all = jnp.concatenate([ka.reshape(batch, seq_len, LANES), ka_c.reshape(batch, ctx_len, LANES)], axis=1)
    v_all = jnp.concatenate([va.reshape(batch, seq_len, LANES), va_c.reshape(batch, ctx_len, LANES)], axis=1)
    att = _attn_call(qa, k_all, v_all, batch=batch, seq_len=seq_len)

    wo = w_out[l]
    wom = wo[:M_W].astype(BF16)
    woa = wo[M_W:].reshape(A_HEADS, 1, A_DH, D)
    grp = (jnp.arange(A_HEADS)[:, None] // A_REP == jnp.arange(A_KV)[None, :]).astype(F32)
    woa = (woa * grp[:, :, None, None]).reshape(A_HEADS, LANES, D).astype(BF16)
    mod_mix = jnp.stack([modl[:, 2], modl[:, 3], modl[:, 4]], axis=1)
    keys = peer_keys[l].reshape(2 * P_HEADS, N_KEYS, P_HALF).astype(BF16)
    x1, st = _mix_call(hf, hb, om, att, xf, mod_mix, wom, woa, mh_norm_w[l].reshape(1, M_W),
                       ln1_g[l].reshape(1, D), ln1_b[l].reshape(1, D), peer_wq[l].astype(BF16), keys, seq_len=seq_len)

    r2, e2, c1, e1 = _topk_call(st)
    mod_peer = jnp.stack([modl[:, 3], modl[:, 4], modl[:, 5]], axis=1)
    out = _peer_call(x1, mod_peer, _pack_call(peer_u[l], transpose=False), _pack_call(peer_v[l], transpose=True),
                     r2, e2, c1, e1,
                     ln2_g[l].reshape(1, D), ln2_b[l].reshape(1, D), seq_len=seq_len)
    return out.reshape(batch, seq_len, D)


def kernel(x, c, ctx, c_ctx, w_mod, b_mod, w_in, conv_w, conv_b, b_gates, mh_norm_w, q_norm_w, k_norm_w, w_out,
           ln1_g, ln1_b, peer_wq, peer_keys, peer_u, peer_v, ln2_g, ln2_b):
    return _forward(x, c, ctx, c_ctx, w_mod, b_mod, w_in, conv_w, conv_b, b_gates, mh_norm_w, q_norm_w, k_norm_w,
                    w_out, ln1_g, ln1_b, peer_wq, peer_keys, peer_u, peer_v, ln2_g, ln2_b)
```

```python
import functools

import jax
import jax.numpy as jnp
from jax import lax
from jax.experimental import pallas as pl
from jax.experimental.pallas import tpu as pltpu

F32 = jnp.float32
BF16 = jnp.bfloat16

D_MODEL = 1024
DEPTH = 1
GRID_W = 64
M_HEADS = 4
M_DH = 128
M_W = M_HEADS * M_DH
M_CHUNK = 128
N_GATES = 4 * M_HEADS
A_HEADS = 8
A_KV = 2
A_REP = A_HEADS // A_KV
A_DH = 64
A_W = A_HEADS * A_DH
A_KVW = A_KV * A_DH
ROPE_THETA = 10000.0
ATTN_SCALE = A_DH ** -0.5
G_END = 4 * M_W + N_GATES
P_HEADS = 8
N_KEYS = 128
P_HALF = 128
P_QDIM = 2 * P_HALF
P_TOPK = 16
N_MOD = 6
DEEPNORM_ALPHA = (2.0 * DEPTH) ** 0.25
LN_EPS = 1e-5
RMS_EPS = 1e-6

LANES = 128
SUBLANES = 8
VMEM_LIMIT = 56 * 1024 * 1024

C_QK = 0
C_VM = 2 * M_W
C_OM = 3 * M_W
C_G = 4 * M_W
C_QA = C_G + LANES
C_KA = C_QA + A_HEADS * LANES
C_VA = C_KA + LANES
C_END = C_VA + LANES

NEG_INF = float("-inf")


def _cparams(sem):
    return pltpu.CompilerParams(dimension_semantics=sem, vmem_limit_bytes=VMEM_LIMIT)


def _dot(a, b):
    return jnp.dot(a, b, preferred_element_type=F32)


def _dot_nt(a, b):
    return lax.dot_general(a, b, (((1,), (1,)), ((), ())), preferred_element_type=F32)


def _dot_tn(a, b):
    return lax.dot_general(a, b, (((0,), (0,)), ((), ())), preferred_element_type=F32)


def _split3(x):
    hi = x.astype(BF16)
    r = x - hi.astype(F32)
    mid = r.astype(BF16)
    lo = (r - mid.astype(F32)).astype(BF16)
    return hi, mid, lo


def _pack_bf16(x):
    return pltpu.bitcast(x.astype(BF16), jnp.uint32)


def _unpack_bf16(x):
    return pltpu.bitcast(x, BF16)


def _sigmoid(x):
    return 1.0 / (1.0 + jnp.exp(-x))


def _log_sigmoid(x):
    return jnp.minimum(x, 0.0) - jnp.log(1.0 + jnp.exp(-jnp.abs(x)))


def _layer_norm(y, g, b):
    mu = jnp.mean(y, axis=-1, keepdims=True)
    yc = y - mu
    var = jnp.mean(yc * yc, axis=-1, keepdims=True)
    return yc * lax.rsqrt(var + LN_EPS) * g + b


def _mod_body(c_ref, w_ref, b_ref, o_ref):
    c = c_ref[...]
    a = c * _sigmoid(c)
    hi, mid, lo = _split3(a)
    w = w_ref[...]
    whi = w.astype(BF16)
    wlo = (w - whi.astype(F32)).astype(BF16)
    o_ref[...] = (_dot(hi, whi) + _dot(mid, whi) + _dot(hi, wlo) + _dot(lo, whi) + _dot(mid, wlo)) + b_ref[...]


def _mod_call(cc, w, b):
    n = w.shape[1]
    tn = 1536
    return pl.pallas_call(
        _mod_body,
        grid=(n // tn,),
        in_specs=[pl.BlockSpec((SUBLANES, D_MODEL), lambda j: (0, 0)),
                  pl.BlockSpec((D_MODEL, tn), lambda j: (0, j)),
                  pl.BlockSpec((1, tn), lambda j: (0, j))],
        out_specs=pl.BlockSpec((SUBLANES, tn), lambda j: (0, j)),
        out_shape=jax.ShapeDtypeStruct((SUBLANES, n), F32),
        compiler_params=_cparams(("arbitrary",)),
        name="mod",
    )(cc, w, b.reshape(1, n))


def _proj_body(x_ref, xp_ref, xn_ref, mod_ref, w_ref, wgt_ref, cw_ref, cb_ref, bgr_ref, bgc_ref,
               nq_ref, nk_ref, cos_ref, sin_ref,
               qm_ref, km_ref, vm_ref, om_ref, g_ref, gt_ref, qa_ref, ka_ref, va_ref,
               *, tl, tiles_per_seq, rope):
    t = lax.rem(pl.program_id(0), tiles_per_seq)
    shift = mod_ref[0, 0:1, :]
    scale1 = 1.0 + mod_ref[0, 1:2, :]

    def modulate(v):
        return (v * scale1 + shift).astype(BF16)

    h = modulate(x_ref[...])
    z = _dot(h, w_ref[...])
    wqk = w_ref[:, C_QK:C_VM]
    zp = _dot(modulate(xp_ref[...]), wqk)[SUBLANES - 1:SUBLANES, :]
    zn = _dot(modulate(xn_ref[...]), wqk)[0:1, :]
    zp = jnp.where(t != 0, zp, 0.0)
    zn = jnp.where(t != tiles_per_seq - 1, zn, 0.0)
    zqk = z[:, C_QK:C_VM]
    rows = lax.broadcasted_iota(jnp.int32, (tl, 1), 0)
    z_dn = jnp.where(rows == 0, zp, pltpu.roll(zqk, 1, 0))
    z_up = jnp.where(rows == tl - 1, zn, pltpu.roll(zqk, tl - 1, 0))
    y = cw_ref[0:1, :] * z_dn + cw_ref[1:2, :] * zqk + cw_ref[2:3, :] * z_up + cb_ref[...]
    y = y * _sigmoid(y)
    qm_ref[...] = (y[:, :M_W] * (M_DH ** -0.5)).astype(BF16)
    km_ref[...] = y[:, M_W:].astype(BF16)
    vm_ref[...] = z[:, C_VM:C_OM].astype(BF16)
    om_ref[...] = z[:, C_OM:C_G]
    g_ref[...] = z[:, C_G:C_QA] + bgr_ref[...]
    gt_ref[...] = _dot_nt(wgt_ref[...], h) + bgc_ref[...]

    lane = lax.broadcasted_iota(jnp.int32, (1, LANES), 1)
    if rope:
        cos = cos_ref[...]
        sin = sin_ref[...]
        lo = lax.rem(lane, 32) < 16

    def rope_fn(v):
        if not rope:
            return v
        partner = jnp.where(lo, pltpu.roll(v, LANES - 16, 1), pltpu.roll(v, 16, 1))
        return v * cos + partner * sin

    for hh in range(A_HEADS):
        zq = z[:, C_QA + LANES * hh:C_QA + LANES * (hh + 1)]
        ms = jnp.sum(zq * zq, axis=-1, keepdims=True) * (1.0 / A_DH)
        qn = zq * lax.rsqrt(ms + RMS_EPS) * nq_ref[...]
        qa_ref[hh] = (rope_fn(qn) * ATTN_SCALE).astype(BF16)
    zk = z[:, C_KA:C_VA]
    sq = zk * zk
    first = lane < A_DH
    s0 = jnp.sum(jnp.where(first, sq, 0.0), axis=-1, keepdims=True)
    s1 = jnp.sum(jnp.where(first, 0.0, sq), axis=-1, keepdims=True)
    ms = jnp.where(first, s0, s1) * (1.0 / A_DH)
    kn = zk * lax.rsqrt(ms + RMS_EPS) * nk_ref[...]
    ka_ref[...] = rope_fn(kn).astype(BF16)
    va_ref[...] = z[:, C_VA:C_END].astype(BF16)


def _proj_call(xf, mod2, w_all, wgt, cw, cb, bgr, bgc, nq, nk, cos, sin, *, seq_len, rope):
    n = xf.shape[0]
    tl = min(512, seq_len)
    tps = seq_len // tl
    nt = n // tl
    hb = tl // SUBLANES
    nblk8 = n // SUBLANES
    const = lambda i: (0, 0)
    body = functools.partial(_proj_body, tl=tl, tiles_per_seq=tps, rope=rope)
    out_shapes = (
        jax.ShapeDtypeStruct((n, M_W), BF16),
        jax.ShapeDtypeStruct((n, M_W), BF16),
        jax.ShapeDtypeStruct((n, M_W), BF16),
        jax.ShapeDtypeStruct((n, M_W), F32),
        jax.ShapeDtypeStruct((n, LANES), F32),
        jax.ShapeDtypeStruct((N_GATES, n), F32),
        jax.ShapeDtypeStruct((A_HEADS, n, LANES), BF16),
        jax.ShapeDtypeStruct((n, LANES), BF16),
        jax.ShapeDtypeStruct((n, LANES), BF16),
    )
    row = lambda i: (i, 0)
    out_specs = (
        pl.BlockSpec((tl, M_W), row), pl.BlockSpec((tl, M_W), row), pl.BlockSpec((tl, M_W), row),
        pl.BlockSpec((tl, M_W), row), pl.BlockSpec((tl, LANES), row),
        pl.BlockSpec((N_GATES, tl), lambda i: (0, i)),
        pl.BlockSpec((A_HEADS, tl, LANES), lambda i: (0, i, 0)),
        pl.BlockSpec((tl, LANES), row), pl.BlockSpec((tl, LANES), row),
    )
    in_specs = [
        pl.BlockSpec((tl, D_MODEL), row),
        pl.BlockSpec((SUBLANES, D_MODEL), lambda i: (jnp.maximum(i * hb - 1, 0), 0)),
        pl.BlockSpec((SUBLANES, D_MODEL), lambda i: (jnp.minimum((i + 1) * hb, nblk8 - 1), 0)),
        pl.BlockSpec((1, 2, D_MODEL), lambda i: (i // tps if mod2.shape[0] > 1 else 0, 0, 0)),
        pl.BlockSpec((D_MODEL, C_END), const),
        pl.BlockSpec((N_GATES, D_MODEL), const),
        pl.BlockSpec((3, 2 * M_W), const),
        pl.BlockSpec((1, 2 * M_W), const),
        pl.BlockSpec((1, LANES), const),
        pl.BlockSpec((N_GATES, 1), const),
        pl.BlockSpec((1, LANES), const),
        pl.BlockSpec((1, LANES), const),
        pl.BlockSpec((tl, LANES), lambda i: (lax.rem(i, tps), 0)),
        pl.BlockSpec((tl, LANES), lambda i: (lax.rem(i, tps), 0)),
    ]
    return pl.pallas_call(
        body, grid=(nt,), in_specs=in_specs, out_specs=out_specs, out_shape=out_shapes,
        compiler_params=_cparams(("arbitrary",)), name="proj_rope" if rope else "proj_ctx",
    )(xf, xf, xf, mod2, w_all, wgt, cw, cb, bgr, bgc, nq, nk, cos, sin)


def _mlstm_body(qf_ref, kf_ref, vf_ref, gf_ref, gtf_ref, qb_ref, kb_ref, vb_ref, gb_ref, gtb_ref,
                ct0_ref, m0_ref, hf_ref, hb_ref, cto_ref, mo_ref, ct_s, m_s, *, sub):
    c = pl.program_id(1)
    nc = pl.num_programs(1)
    T = M_CHUNK

    @pl.when(c == 0)
    def _():
        ct_s[...] = ct0_ref[0]
        m_s[...] = m0_ref[0]

    ri = lax.broadcasted_iota(jnp.int32, (T, T), 0)
    ci = lax.broadcasted_iota(jnp.int32, (T, T), 1)
    le = ri <= ci
    ge = ri >= ci
    u_le = jnp.where(le, 1.0, 0.0).astype(BF16)
    u_ge = jnp.where(ge, 1.0, 0.0).astype(BF16)
    ones_col = jnp.where(ci == 0, 1.0, 0.0).astype(BF16)

    dirs = ((qf_ref, kf_ref, vf_ref, gf_ref, gtf_ref, hf_ref, ge, u_le, u_ge),
            (qb_ref, kb_ref, vb_ref, gb_ref, gtb_ref, hb_ref, le, u_ge, u_le))
    for d, (q_ref, k_ref, v_ref, g_ref, gt_ref, h_ref, mask, u_row, u_col) in enumerate(dirs):
        ct = [ct_s[M_HEADS * d + h] for h in range(M_HEADS)]
        m_run = [m_s[M_HEADS * d + h:M_HEADS * d + h + 1, 0:1] for h in range(M_HEADS)]
        for sc in (range(sub) if d == 0 else range(sub - 1, -1, -1)):
            rows = slice(T * sc, T * (sc + 1))
            grp = gt_ref[2 * M_HEADS * d:2 * M_HEADS * (d + 1), rows]
            lf8 = _log_sigmoid(grp)
            a, b_, c_ = _split3(lf8)
            b8 = _dot(a, u_row) + _dot(b_, u_row) + _dot(c_, u_row)
            gcol = g_ref[rows, :]
            a, b_, c_ = _split3(_log_sigmoid(gcol))
            bcol = _dot(u_col, a) + _dot(u_col, b_) + _dot(u_col, c_)
            for h in range(M_HEADS):
                g_row = grp[h:h + 1, :] - b8[M_HEADS + h:M_HEADS + h + 1, :]
                li_col = gcol[:, 2 * M_HEADS * d + h:2 * M_HEADS * d + h + 1]
                b_col = bcol[:, 2 * M_HEADS * d + M_HEADS + h:2 * M_HEADS * d + M_HEADS + h + 1]
                g_col = li_col - b_col
                m_prev = m_run[h]
                d0 = jnp.where(mask, g_row, NEG_INF)
                mx = jnp.maximum(m_prev, jnp.max(d0, axis=1, keepdims=True))
                dmat = jnp.exp(d0 - mx)
                sl = slice(M_DH * h, M_DH * (h + 1))
                q = q_ref[rows, sl]
                k = k_ref[rows, sl]
                v = v_ref[rows, sl]
                s = _dot_nt(q, k) * dmat
                vaug = jnp.concatenate([v, ones_col], axis=1)
                num = _dot(s.astype(BF16), vaug) + jnp.exp(m_prev - mx) * _dot(q, ct[h].astype(BF16))
                den = num[:, M_DH:M_DH + 1]
                floor = jnp.exp(-(b_col + mx))
                h_ref[rows, sl] = num[:, :M_DH] / jnp.maximum(jnp.abs(den), floor)
                mxl = jnp.maximum(m_prev, jnp.max(g_row, axis=1, keepdims=True))
                b_last = jnp.sum(lf8[M_HEADS + h:M_HEADS + h + 1, :], axis=1, keepdims=True)
                w_col = jnp.exp(g_col - mxl)
                wv = (w_col * vaug.astype(F32)).astype(BF16)
                ct[h] = jnp.exp(m_prev - mxl) * ct[h] + _dot_tn(k, wv)
                m_run[h] = b_last + mxl
        for h in range(M_HEADS):
            ch = M_HEADS * d + h
            ct_s[ch] = ct[h]
            m_s[ch:ch + 1, :] = jnp.broadcast_to(m_run[h], (1, LANES))

    @pl.when(c == nc - 1)
    def _():
        cto_ref[0] = ct_s[...]
        mo_ref[0] = m_s[...]


def _mlstm_call(qm, km, vm, g, gt, ct0, m0, *, batch, seq_len):
    n = qm.shape[0]
    sub = min(4, seq_len // M_CHUNK)
    rows = sub * M_CHUNK
    nc = seq_len // rows
    nch = 2 * M_HEADS
    fwd = lambda b, c: (b * nc + c, 0)
    bwd = lambda b, c: (b * nc + nc - 1 - c, 0)
    fwd_t = lambda b, c: (0, b * nc + c)
    bwd_t = lambda b, c: (0, b * nc + nc - 1 - c)
    tile = lambda im: pl.BlockSpec((rows, M_W), im)
    in_specs = [tile(fwd), tile(fwd), tile(fwd), pl.BlockSpec((rows, LANES), fwd), pl.BlockSpec((N_GATES, rows), fwd_t),
                tile(bwd), tile(bwd), tile(bwd), pl.BlockSpec((rows, LANES), bwd), pl.BlockSpec((N_GATES, rows), bwd_t),
                pl.BlockSpec((1, nch, M_DH, 2 * M_DH), lambda b, c: (b, 0, 0, 0)),
                pl.BlockSpec((1, nch, LANES), lambda b, c: (b, 0, 0))]
    out_specs = (tile(fwd), tile(bwd),
                 pl.BlockSpec((1, nch, M_DH, 2 * M_DH), lambda b, c: (b, 0, 0, 0)),
                 pl.BlockSpec((1, nch, LANES), lambda b, c: (b, 0, 0)))
    out_shape = (jax.ShapeDtypeStruct((n, M_W), F32), jax.ShapeDtypeStruct((n, M_W), F32),
                 jax.ShapeDtypeStruct((batch, nch, M_DH, 2 * M_DH), F32),
                 jax.ShapeDtypeStruct((batch, nch, LANES), F32))
    return pl.pallas_call(
        functools.partial(_mlstm_body, sub=sub), grid=(batch, nc), in_specs=in_specs, out_specs=out_specs,
        out_shape=out_shape,
        scratch_shapes=[pltpu.VMEM((nch, M_DH, 2 * M_DH), F32), pltpu.VMEM((nch, LANES), F32)],
        compiler_params=_cparams(("arbitrary", "arbitrary")), name="mlstm",
    )(qm, km, vm, g, gt, qm, km, vm, g, gt, ct0, m0)


def _attn_body(q_ref, k_ref, v_ref, o_ref, *, tk, n_keys):
    ci = lax.broadcasted_iota(jnp.int32, (tk, LANES), 1)
    ones_col = jnp.where(ci == 0, 1.0, 0.0).astype(BF16)
    m = [None] * A_HEADS
    acc = [None] * A_HEADS
    for j in range(n_keys // tk):
        kt = k_ref[0, tk * j:tk * (j + 1), :]
        vaug = jnp.concatenate([v_ref[0, tk * j:tk * (j + 1), :], ones_col], axis=1)
        for h in range(A_HEADS):
            s = _dot_nt(q_ref[h], kt)
            row_max = jnp.max(s, axis=1, keepdims=True)
            if j == 0:
                m_new = row_max
                acc[h] = _dot(jnp.exp(s - m_new).astype(BF16), vaug)
            else:
                m_new = jnp.maximum(m[h], row_max)
                acc[h] = jnp.exp(m[h] - m_new) * acc[h] + _dot(jnp.exp(s - m_new).astype(BF16), vaug)
            m[h] = m_new
    for h in range(A_HEADS):
        o_ref[h] = (acc[h][:, :LANES] / acc[h][:, LANES:LANES + 1]).astype(BF16)


def _attn_call(qa, k_all, v_all, *, batch, seq_len):
    n = qa.shape[1]
    n_keys = k_all.shape[1]
    tq = 256
    tk = next(t for t in (2816, 1408, 768, 256) if n_keys % t == 0)
    nq = seq_len // tq
    body = functools.partial(_attn_body, tk=tk, n_keys=n_keys)
    return pl.pallas_call(
        body, grid=(batch, nq),
        in_specs=[pl.BlockSpec((A_HEADS, tq, LANES), lambda b, i: (0, b * nq + i, 0)),
                  pl.BlockSpec((1, n_keys, LANES), lambda b, i: (b, 0, 0)),
                  pl.BlockSpec((1, n_keys, LANES), lambda b, i: (b, 0, 0))],
        out_specs=pl.BlockSpec((A_HEADS, tq, LANES), lambda b, i: (0, b * nq + i, 0)),
        out_shape=jax.ShapeDtypeStruct((A_HEADS, n, LANES), BF16),
        compiler_params=_cparams(("arbitrary", "arbitrary")), name="attn",
    )(qa, k_all, v_all)


def _mix_body(hf_ref, hb_ref, om_ref, att_ref, x_ref, mod_ref, wom_ref, woa_ref, mhw_ref, lg_ref, lb_ref,
              wq_ref, keys_ref, x1_ref, st_ref):
    hsum = hf_ref[...] + hb_ref[...]
    om = om_ref[...]
    parts = []
    for h in range(M_HEADS):
        sl = slice(M_DH * h, M_DH * (h + 1))
        blk = hsum[:, sl]
        mu = jnp.mean(blk, axis=-1, keepdims=True)
        xc = blk - mu
        var = jnp.mean(xc * xc, axis=-1, keepdims=True)
        hn = xc * lax.rsqrt(var + LN_EPS) * mhw_ref[:, sl]
        parts.append((_sigmoid(om[:, sl]) * hn).astype(BF16))
    mix = _dot(jnp.concatenate(parts, axis=1), wom_ref[...])
    for h in range(A_HEADS):
        mix = mix + _dot(att_ref[h], woa_ref[h])
    g1 = mod_ref[0, 0:1, :]
    sh2 = mod_ref[0, 1:2, :]
    sc2 = mod_ref[0, 2:3, :]
    x1 = _layer_norm(DEEPNORM_ALPHA * x_ref[...] + g1 * mix, lg_ref[...], lb_ref[...])
    x1_ref[...] = x1
    hp = (x1 * (1.0 + sc2) + sh2).astype(BF16)
    qp = _dot(hp, wq_ref[...])
    for j in range(2 * P_HEADS):
        blk = qp[:, P_HALF * j:P_HALF * (j + 1)].astype(BF16)
        st_ref[j] = _dot_nt(keys_ref[j], blk)


def _mix_call(hf, hb, om, att, xf, mod3, wom, woa, mhw, lg, lb, wq, keys, *, seq_len):
    n = xf.shape[0]
    tl = 256
    tps = seq_len // tl
    row = lambda i: (i, 0)
    const2 = lambda i: (0, 0)
    const3 = lambda i: (0, 0, 0)
    return pl.pallas_call(
        _mix_body, grid=(n // tl,),
        in_specs=[pl.BlockSpec((tl, M_W), row), pl.BlockSpec((tl, M_W), row), pl.BlockSpec((tl, M_W), row),
                  pl.BlockSpec((A_HEADS, tl, LANES), lambda i: (0, i, 0)),
                  pl.BlockSpec((tl, D_MODEL), row),
                  pl.BlockSpec((1, 3, D_MODEL), lambda i: (i // tps, 0, 0)),
                  pl.BlockSpec((M_W, D_MODEL), const2),
                  pl.BlockSpec((A_HEADS, LANES, D_MODEL), const3),
                  pl.BlockSpec((1, M_W), const2), pl.BlockSpec((1, D_MODEL), const2), pl.BlockSpec((1, D_MODEL), const2),
                  pl.BlockSpec((D_MODEL, P_HEADS * P_QDIM), const2),
                  pl.BlockSpec((2 * P_HEADS, N_KEYS, P_HALF), const3)],
        out_specs=(pl.BlockSpec((tl, D_MODEL), row),
                   pl.BlockSpec((2 * P_HEADS, N_KEYS, tl), lambda i: (0, 0, i))),
        out_shape=(jax.ShapeDtypeStruct((n, D_MODEL), F32),
                   jax.ShapeDtypeStruct((2 * P_HEADS, N_KEYS, n), F32)),
        compiler_params=_cparams(("arbitrary",)), name="mix",
    )(hf, hb, om, att, xf, mod3, wom, woa, mhw, lg, lb, wq, keys)


def _sublane_all(op, x):
    x = op(x, pltpu.roll(x, 4, 0))
    x = op(x, pltpu.roll(x, 2, 0))
    return op(x, pltpu.roll(x, 1, 0))


def _tree(op, xs):
    xs = list(xs)
    while len(xs) > 1:
        xs = [op(xs[i], xs[i + 1]) if i + 1 < len(xs) else xs[i] for i in range(0, len(xs), 2)]
    return xs[0]


def _extract16(blocks, n_ranked):
    v = list(blocks)
    rank = [None] * len(v)
    vals = []
    for i in range(P_TOPK):
        m = _sublane_all(jnp.maximum, _tree(jnp.maximum, v))
        vals.append(m)
        for kb in range(len(v)):
            hit = v[kb] == m
            if i < n_ranked:
                rank[kb] = jnp.where(hit, float(i), float(N_KEYS) if rank[kb] is None else rank[kb])
            if i + 1 < P_TOPK:
                v[kb] = jnp.where(hit, NEG_INF, v[kb])
    return vals, rank


_LOW_RANKS = 5


def _topk_unit(st_ref, r2_ref, e2_ref, c1_ref, e1_ref, h, lanes):
    nkb = N_KEYS // SUBLANES
    s1 = [st_ref[2 * h, SUBLANES * kb:SUBLANES * (kb + 1), lanes] for kb in range(nkb)]
    s2 = [st_ref[2 * h + 1, SUBLANES * kb:SUBLANES * (kb + 1), lanes] for kb in range(nkb)]
    v1, rank1 = _extract16(s1, _LOW_RANKS)
    v2, rank2 = _extract16(s2, P_TOPK)
    row = lax.broadcasted_iota(jnp.int32, v1[0].shape, 0)

    def column(vals, base):
        out = vals[base + SUBLANES - 1]
        for r in range(SUBLANES - 2, -1, -1):
            out = jnp.where(row == r, vals[base + r], out)
        return out

    s2lo = column(v2, 0)
    s2hi = column(v2, SUBLANES)
    s1hi = column(v1, 6)
    cands = [
        v1[0] + s2lo, v1[0] + s2hi, v1[1] + s2lo,
        jnp.where(row < 5, v1[2] + s2lo, v1[4] + pltpu.roll(s2lo, 5, 0)),
        jnp.where(row < 4, v1[3] + s2lo,
                  jnp.where(row < 6, v1[5] + pltpu.roll(s2lo, 4, 0), v1[6] + pltpu.roll(s2lo, 6, 0))),
        jnp.where(row < 2, v1[7] + s2lo, s1hi + v2[0]),
        jnp.where(row < 2, jnp.where(row == 0, v1[14], v1[15]) + v2[0], NEG_INF),
    ]
    work = list(cands)
    for it in range(P_TOPK):
        tau = _sublane_all(jnp.maximum, _tree(jnp.maximum, work))
        if it + 1 < P_TOPK:
            work = [jnp.where(w == tau, NEG_INF, w) for w in work]
    top = v1[0] + v2[0]
    sel = [c >= tau for c in cands]
    ind = [jnp.where(m, 1.0, 0.0) for m in sel]
    z = _sublane_all(jnp.add, _tree(jnp.add, [jnp.where(m, jnp.exp(c - top), 0.0) for m, c in zip(sel, cands)]))
    cnt = [None] * _LOW_RANKS
    cnt[0] = _sublane_all(jnp.add, ind[0] + ind[1])
    cnt[1] = _sublane_all(jnp.add, ind[2])
    cnt[2] = _sublane_all(jnp.add, jnp.where(row < 5, ind[3], 0.0))
    cnt[4] = _sublane_all(jnp.add, ind[3]) - cnt[2]
    cnt[3] = _sublane_all(jnp.add, jnp.where(row < 4, ind[4], 0.0))
    zinv = 1.0 / z
    for kb in range(nkb):
        rows = slice(SUBLANES * kb, SUBLANES * (kb + 1))
        c1 = jnp.where(s1[kb] + v2[0] >= tau, 1.0, 0.0) + jnp.where(s1[kb] + v2[1] >= tau, 1.0, 0.0)
        for i in range(_LOW_RANKS):
            c1 = jnp.where(rank1[kb] == float(i), cnt[i], c1)
        c1_ref[h, rows, lanes] = c1
        e1_ref[h, rows, lanes] = jnp.exp(s1[kb] - v1[0]) * zinv
    for pb in range(nkb // 2):
        rows = slice(SUBLANES * pb, SUBLANES * (pb + 1))
        r2 = jnp.concatenate([rank2[2 * pb], rank2[2 * pb + 1]], axis=0).astype(BF16)
        e2 = jnp.concatenate([jnp.exp(s2[2 * pb] - v2[0]), jnp.exp(s2[2 * pb + 1] - v2[0])], axis=0).astype(BF16)
        r2_ref[h, rows, lanes] = pltpu.bitcast(r2, jnp.uint32)
        e2_ref[h, rows, lanes] = pltpu.bitcast(e2, jnp.uint32)


def _topk_body(st_ref, r2_ref, e2_ref, c1_ref, e1_ref):
    for h in range(P_HEADS):
        for lt in range(st_ref.shape[2] // LANES):
            _topk_unit(st_ref, r2_ref, e2_ref, c1_ref, e1_ref, h, slice(LANES * lt, LANES * (lt + 1)))


def _topk_call(st):
    n = st.shape[2]
    tt = 256
    spec_in = pl.BlockSpec((2 * P_HEADS, N_KEYS, tt), lambda i: (0, 0, i))
    spec_out = pl.BlockSpec((P_HEADS, N_KEYS, tt), lambda i: (0, 0, i))
    shp = jax.ShapeDtypeStruct((P_HEADS, N_KEYS, n), F32)
    shp16 = jax.ShapeDtypeStruct((P_HEADS, N_KEYS // 2, n), jnp.uint32)
    spec16 = pl.BlockSpec((P_HEADS, N_KEYS // 2, tt), lambda i: (0, 0, i))
    return pl.pallas_call(
        _topk_body, grid=(n // tt,), in_specs=[spec_in], out_specs=(spec16, spec16, spec_out, spec_out),
        out_shape=(shp16, shp16, shp, shp),
        compiler_params=_cparams(("arbitrary",)), name="topk",
    )(st)


def _pack_body(x_ref, o_ref, *, transpose):
    x = x_ref[...]
    o_ref[...] = _pack_bf16(x.T if transpose else x)


def _pack_call(x, *, transpose):
    r, c = x.shape
    tr = 512
    if transpose:
        out_shape = jax.ShapeDtypeStruct((c // 2, r), jnp.uint32)
        out_spec = pl.BlockSpec((c // 2, tr), lambda i: (0, i))
    else:
        out_shape = jax.ShapeDtypeStruct((r // 2, c), jnp.uint32)
        out_spec = pl.BlockSpec((tr // 2, c), lambda i: (i, 0))
    return pl.pallas_call(
        functools.partial(_pack_body, transpose=transpose), grid=(r // tr,),
        in_specs=[pl.BlockSpec((tr, c), lambda i: (i, 0))], out_specs=out_spec, out_shape=out_shape,
        compiler_params=_cparams(("arbitrary",)), name="pack_t" if transpose else "pack",
    )(x)


PEER_CHUNK = 2 * N_KEYS
BF16_ROWS = 16


def _peer_gate_stage(at_ref, at_off, h_ref, r2_ref, e2_ref, c1_ref, e1_ref, a0, tt):
    for k in range(PEER_CHUNK // N_KEYS):
        for j in range(tt // LANES):
            cols = slice(LANES * j, LANES * (j + 1))
            c1b = [jnp.broadcast_to(c1_ref[h, a0 + k:a0 + k + 1, cols], (BF16_ROWS, LANES)).astype(BF16)
                   for h in range(P_HEADS)]
            e1b = [jnp.broadcast_to(e1_ref[h, a0 + k:a0 + k + 1, cols], (BF16_ROWS, LANES)).astype(BF16)
                   for h in range(P_HEADS)]
            for rb in range(N_KEYS // BF16_ROWS):
                words = slice(SUBLANES * rb, SUBLANES * (rb + 1))
                hrows = slice((N_KEYS * k) // 2 + SUBLANES * rb, (N_KEYS * k) // 2 + SUBLANES * (rb + 1))
                w = None
                for h in range(P_HEADS):
                    e2v = pltpu.bitcast(e2_ref[h, words, cols], BF16)
                    r2v = pltpu.bitcast(r2_ref[h, words, cols], BF16)
                    term = jnp.where(r2v < c1b[h], e2v, jnp.zeros_like(e2v)) * e1b[h]
                    w = term if w is None else w + term
                arows = slice(at_off + hrows.start, at_off + hrows.stop)
                a = _unpack_bf16(at_ref[arows, cols])
                act = (0.5 * a) * (1.0 + lax.erf(a * (2.0 ** -0.5)))
                h_ref[hrows, cols] = pltpu.bitcast(act * w, jnp.uint32)


def _peer_body(x1_ref, mod_ref, u_ref, vt_ref, r2_ref, e2_ref, c1_ref, e1_ref, lg_ref, lb_ref,
               o_ref, hpt_s, acc_s, at0_s, at1_s, h0_s, h1_s, *, et, tt):
    e = pl.program_id(1)

    @pl.when(e == 0)
    def _():
        sh2 = mod_ref[0, 0:1, :]
        sc2 = mod_ref[0, 1:2, :]
        hpt_s[...] = _pack_bf16((x1_ref[...] * (1.0 + sc2) + sh2).T)
        acc_s[...] = jnp.zeros(acc_s.shape, F32)

    at_bufs = (at0_s, at1_s)
    h_bufs = (h0_s, h1_s)
    nch = et // PEER_CHUNK

    def rows(c):
        return slice(PEER_CHUNK * c, PEER_CHUNK * (c + 1))

    def stage_a(c2):
        prows = slice(PEER_CHUNK * c2, PEER_CHUNK * (c2 + 1))
        return _pack_bf16(_dot(_unpack_bf16(u_ref[prows, :]), _unpack_bf16(hpt_s[...])))

    at_bufs[0][...] = stage_a(0)
    for c2 in range(nch // 2):
        if c2 + 1 < nch // 2:
            at_bufs[(c2 + 1) % 2][...] = stage_a(c2 + 1)
        for half in range(2):
            c = 2 * c2 + half
            _peer_gate_stage(at_bufs[c2 % 2], half * (PEER_CHUNK // 2), h_bufs[half], r2_ref, e2_ref, c1_ref, e1_ref,
                             c * (PEER_CHUNK // N_KEYS), tt)
            acc_s[...] += _dot(_unpack_bf16(vt_ref[:, rows(c)]), _unpack_bf16(h_bufs[half][...]))

    @pl.when(e == pl.num_programs(1) - 1)
    def _():
        g2 = mod_ref[0, 2:3, :]
        y = DEEPNORM_ALPHA * x1_ref[...] + g2 * acc_s[...].T
        o_ref[...] = _layer_norm(y, lg_ref[...], lb_ref[...])


def _peer_call(x1, mod3, u, v, r2, e2, c1, e1, lg, lb, *, seq_len):
    n = x1.shape[0]
    n_exp = 2 * u.shape[0]
    tt = min(1024, seq_len)
    et = 2048
    tps = seq_len // tt
    na = et // N_KEYS
    body = functools.partial(_peer_body, et=et, tt=tt)
    tok = lambda t, e: (t, 0)
    const2 = lambda t, e: (0, 0)
    once = pl.Buffered(1)
    full = pl.BlockSpec((P_HEADS, N_KEYS // 2, tt), lambda t, e: (0, 0, t), pipeline_mode=once)
    part = pl.BlockSpec((P_HEADS, na, tt), lambda t, e: (0, e, t))
    return pl.pallas_call(
        body, grid=(n // tt, n_exp // et),
        in_specs=[pl.BlockSpec((tt, D_MODEL), tok, pipeline_mode=once),
                  pl.BlockSpec((1, 3, D_MODEL), lambda t, e: (t // tps, 0, 0)),
                  pl.BlockSpec((et // 2, D_MODEL), lambda t, e: (e, 0)),
                  pl.BlockSpec((D_MODEL // 2, et), lambda t, e: (0, e)),
                  full, full, part, part,
                  pl.BlockSpec((1, D_MODEL), const2), pl.BlockSpec((1, D_MODEL), const2)],
        out_specs=pl.BlockSpec((tt, D_MODEL), tok),
        out_shape=jax.ShapeDtypeStruct((n, D_MODEL), F32),
        scratch_shapes=[pltpu.VMEM((D_MODEL // 2, tt), jnp.uint32), pltpu.VMEM((D_MODEL, tt), F32),
                        pltpu.VMEM((PEER_CHUNK, tt), jnp.uint32), pltpu.VMEM((PEER_CHUNK, tt), jnp.uint32),
                        pltpu.VMEM((PEER_CHUNK // 2, tt), jnp.uint32), pltpu.VMEM((PEER_CHUNK // 2, tt), jnp.uint32)],
        compiler_params=_cparams(("arbitrary", "arbitrary")), name="peer",
    )(x1, mod3, u, v, r2, e2, c1, e1, lg, lb)


def _rope_tables(n_tok):
    pos = jnp.arange(n_tok, dtype=jnp.int32)
    row = (pos // GRID_W).astype(F32)
    col = (pos % GRID_W).astype(F32)
    d = jnp.arange(A_DH)
    freq = ROPE_THETA ** (-(d % 16).astype(F32) / 16.0)
    p = jnp.where(d[None, :] < A_DH // 2, row[:, None], col[:, None])
    ang = p * freq[None, :]
    sign = jnp.where((d % 32) < 16, -1.0, 1.0)
    cos = jnp.tile(jnp.cos(ang), (1, LANES // A_DH))
    sin = jnp.tile(jnp.sin(ang) * sign[None, :], (1, LANES // A_DH))
    return cos, sin


def _proj_weights(w_in, b_gates, q_norm_w, k_norm_w):
    pad_g = jnp.zeros((D_MODEL, LANES - N_GATES), F32)
    wq = w_in[:, G_END:G_END + A_W].reshape(D_MODEL, A_HEADS, 1, A_DH)
    grp = (jnp.arange(A_HEADS)[:, None] // A_REP == jnp.arange(A_KV)[None, :]).astype(F32)
    wq_pad = (wq * grp[None, :, :, None]).reshape(D_MODEL, A_HEADS * LANES)
    w_all = jnp.concatenate([w_in[:, :4 * M_W], w_in[:, 4 * M_W:G_END], pad_g, wq_pad,
                             w_in[:, G_END + A_W:G_END + A_W + A_KVW], w_in[:, G_END + A_W + A_KVW:]], axis=1).astype(BF16)
    wgt = w_in[:, 4 * M_W:G_END].T.astype(BF16)
    bgr = jnp.concatenate([b_gates, jnp.zeros((LANES - N_GATES,), F32)]).reshape(1, LANES)
    bgc = b_gates.reshape(N_GATES, 1)
    nq = jnp.tile(q_norm_w, LANES // A_DH).reshape(1, LANES)
    nk = jnp.tile(k_norm_w, LANES // A_DH).reshape(1, LANES)
    return w_all, wgt, bgr, bgc, nq, nk


def _forward(x, c, ctx, c_ctx, w_mod, b_mod, w_in, conv_w, conv_b, b_gates, mh_norm_w, q_norm_w, k_norm_w,
             w_out, ln1_g, ln1_b, peer_wq, peer_keys, peer_u, peer_v, ln2_g, ln2_b):
    batch, seq_len, _ = x.shape
    ctx_len = ctx.shape[1]
    l = 0
    D = D_MODEL

    cc = jnp.zeros((SUBLANES, D), F32).at[:batch].set(c).at[batch].set(c_ctx)
    mod = _mod_call(cc, w_mod[l], b_mod[l])
    modl = mod[:batch].reshape(batch, N_MOD, D)
    modc = mod[batch:batch + 1].reshape(1, N_MOD, D)

    w_all, wgt, bgr, bgc, nq, nk = _proj_weights(w_in[l], b_gates[l], q_norm_w[l], k_norm_w[l])
    cw = conv_w[l]
    cb = conv_b[l].reshape(1, 2 * M_W)
    cos, sin = _rope_tables(seq_len)

    xf = x.reshape(batch * seq_len, D)
    cf = ctx.reshape(batch * ctx_len, D)
    proj = functools.partial(_proj_call, w_all=w_all, wgt=wgt, cw=cw, cb=cb, bgr=bgr, bgc=bgc, nq=nq, nk=nk)
    ctl = min(512, ctx_len)
    (qm_c, km_c, vm_c, _, g_c, gt_c, _, ka_c, va_c) = proj(
        cf, modc[:, 0:2], cos=cos[:ctl], sin=sin[:ctl], seq_len=ctx_len, rope=False)
    (qm, km, vm, om, g, gt, qa, ka, va) = proj(xf, modl[:, 0:2], cos=cos, sin=sin, seq_len=seq_len, rope=True)

    nch = 2 * M_HEADS
    ct0 = jnp.zeros((batch, nch, M_DH, 2 * M_DH), F32)
    m0 = jnp.zeros((batch, nch, LANES), F32)
    _, _, ct_c, m_c = _mlstm_call(qm_c, km_c, vm_c, g_c, gt_c, ct0, m0, batch=batch, seq_len=ctx_len)
    hf, hb, _, _ = _mlstm_call(qm, km, vm, g, gt, ct_c, m_c, batch=batch, seq_len=seq_len)

    k_all = jnp.concatenate([ka.reshape(batch, seq_len, LANES), ka_c.reshape(batch, ctx_len, LANES)], axis=1)
    v_all = jnp.concatenate([va.reshape(batch, seq_len, LANES), va_c.reshape(batch, ctx_len, LANES)], axis=1)
    att = _attn_call(qa, k_all, v_all, batch=batch, seq_len=seq_len)

    wo = w_out[l]
    wom = wo[:M_W].astype(BF16)
    woa = wo[M_W:].reshape(A_HEADS, 1, A_DH, D)
    grp = (jnp.arange(A_HEADS)[:, None] // A_REP == jnp.arange(A_KV)[None, :]).astype(F32)
    woa = (woa * grp[:, :, None, None]).reshape(A_HEADS, LANES, D).astype(BF16)
    mod_mix = jnp.stack([modl[:, 2], modl[:, 3], modl[:, 4]], axis=1)
    keys = peer_keys[l].reshape(2 * P_HEADS, N_KEYS, P_HALF).astype(BF16)
    x1, st = _mix_call(hf, hb, om, att, xf, mod_mix, wom, woa, mh_norm_w[l].reshape(1, M_W),
                       ln1_g[l].reshape(1, D), ln1_b[l].reshape(1, D), peer_wq[l].astype(BF16), keys, seq_len=seq_len)

    r2, e2, c1, e1 = _topk_call(st)
    mod_peer = jnp.stack([modl[:, 3], modl[:, 4], modl[:, 5]], axis=1)
    out = _peer_call(x1, mod_peer, _pack_call(peer_u[l], transpose=False), _pack_call(peer_v[l], transpose=True),
                     r2, e2, c1, e1,
                     ln2_g[l].reshape(1, D), ln2_b[l].reshape(1, D), seq_len=seq_len)
    return out.reshape(batch, seq_len, D)


def kernel(x, c, ctx, c_ctx, w_mod, b_mod, w_in, conv_w, conv_b, b_gates, mh_norm_w, q_norm_w, k_norm_w, w_out,
           ln1_g, ln1_b, peer_wq, peer_keys, peer_u, peer_v, ln2_g, ln2_b):
    return _forward(x, c, ctx, c_ctx, w_mod, b_mod, w_in, conv_w, conv_b, b_gates, mh_norm_w, q_norm_w, k_norm_w,
                    w_out, ln1_g, ln1_b, peer_wq, peer_keys, peer_u, peer_v, ln2_g, ln2_b)
```

```python
import functools

import jax
import jax.numpy as jnp
from jax import lax
from jax.experimental import pallas as pl
from jax.experimental.pallas import tpu as pltpu

F32 = jnp.float32
BF16 = jnp.bfloat16

D_MODEL = 1024
DEPTH = 1
GRID_W = 64
M_HEADS = 4
M_DH = 128
M_W = M_HEADS * M_DH
M_CHUNK = 128
N_GATES = 4 * M_HEADS
A_HEADS = 8
A_KV = 2
A_REP = A_HEADS // A_KV
A_DH = 64
A_W = A_HEADS * A_DH
A_KVW = A_KV * A_DH
ROPE_THETA = 10000.0
ATTN_SCALE = A_DH ** -0.5
G_END = 4 * M_W + N_GATES
P_HEADS = 8
N_KEYS = 128
P_HALF = 128
P_QDIM = 2 * P_HALF
P_TOPK = 16
N_MOD = 6
DEEPNORM_ALPHA = (2.0 * DEPTH) ** 0.25
LN_EPS = 1e-5
RMS_EPS = 1e-6

LANES = 128
SUBLANES = 8
VMEM_LIMIT = 56 * 1024 * 1024

C_QK = 0
C_VM = 2 * M_W
C_OM = 3 * M_W
C_G = 4 * M_W
C_QA = C_G + LANES
C_KA = C_QA + A_HEADS * LANES
C_VA = C_KA + LANES
C_END = C_VA + LANES

NEG_INF = float("-inf")


def _cparams(sem):
    return pltpu.CompilerParams(dimension_semantics=sem, vmem_limit_bytes=VMEM_LIMIT)


def _dot(a, b):
    return jnp.dot(a, b, preferred_element_type=F32)


def _dot_nt(a, b):
    return lax.dot_general(a, b, (((1,), (1,)), ((), ())), preferred_element_type=F32)


def _dot_tn(a, b):
    return lax.dot_general(a, b, (((0,), (0,)), ((), ())), preferred_element_type=F32)


def _split3(x):
    hi = x.astype(BF16)
    r = x - hi.astype(F32)
    mid = r.astype(BF16)
    lo = (r - mid.astype(F32)).astype(BF16)
    return hi, mid, lo


def _pack_bf16(x):
    return pltpu.bitcast(x.astype(BF16), jnp.uint32)


def _unpack_bf16(x):
    return pltpu.bitcast(x, BF16)


def _sigmoid(x):
    return 1.0 / (1.0 + jnp.exp(-x))


def _log_sigmoid(x):
    return jnp.minimum(x, 0.0) - jnp.log(1.0 + jnp.exp(-jnp.abs(x)))


def _layer_norm(y, g, b):
    mu = jnp.mean(y, axis=-1, keepdims=True)
    yc = y - mu
    var = jnp.mean(yc * yc, axis=-1, keepdims=True)
    return yc * lax.rsqrt(var + LN_EPS) * g + b


def _mod_body(c_ref, w_ref, b_ref, o_ref):
    c = c_ref[...]
    a = c * _sigmoid(c)
    hi, mid, lo = _split3(a)
    w = w_ref[...]
    whi = w.astype(BF16)
    wlo = (w - whi.astype(F32)).astype(BF16)
    o_ref[...] = (_dot(hi, whi) + _dot(mid, whi) + _dot(hi, wlo) + _dot(lo, whi) + _dot(mid, wlo)) + b_ref[...]


def _mod_call(cc, w, b):
    n = w.shape[1]
    tn = 1536
    return pl.pallas_call(
        _mod_body,
        grid=(n // tn,),
        in_specs=[pl.BlockSpec((SUBLANES, D_MODEL), lambda j: (0, 0)),
                  pl.BlockSpec((D_MODEL, tn), lambda j: (0, j)),
                  pl.BlockSpec((1, tn), lambda j: (0, j))],
        out_specs=pl.BlockSpec((SUBLANES, tn), lambda j: (0, j)),
        out_shape=jax.ShapeDtypeStruct((SUBLANES, n), F32),
        compiler_params=_cparams(("arbitrary",)),
        name="mod",
    )(cc, w, b.reshape(1, n))


def _proj_body(x_ref, xp_ref, xn_ref, mod_ref, w_ref, wgt_ref, cw_ref, cb_ref, bgr_ref, bgc_ref,
               nq_ref, nk_ref, cos_ref, sin_ref,
               qm_ref, km_ref, vm_ref, om_ref, g_ref, gt_ref, qa_ref, ka_ref, va_ref,
               *, tl, tiles_per_seq, rope):
    t = lax.rem(pl.program_id(0), tiles_per_seq)
    shift = mod_ref[0, 0:1, :]
    scale1 = 1.0 + mod_ref[0, 1:2, :]

    def modulate(v):
        return (v * scale1 + shift).astype(BF16)

    h = modulate(x_ref[...])
    z = _dot(h, w_ref[...])
    wqk = w_ref[:, C_QK:C_VM]
    zp = _dot(modulate(xp_ref[...]), wqk)[SUBLANES - 1:SUBLANES, :]
    zn = _dot(modulate(xn_ref[...]), wqk)[0:1, :]
    zp = jnp.where(t != 0, zp, 0.0)
    zn = jnp.where(t != tiles_per_seq - 1, zn, 0.0)
    zqk = z[:, C_QK:C_VM]
    rows = lax.broadcasted_iota(jnp.int32, (tl, 1), 0)
    z_dn = jnp.where(rows == 0, zp, pltpu.roll(zqk, 1, 0))
    z_up = jnp.where(rows == tl - 1, zn, pltpu.roll(zqk, tl - 1, 0))
    y = cw_ref[0:1, :] * z_dn + cw_ref[1:2, :] * zqk + cw_ref[2:3, :] * z_up + cb_ref[...]
    y = y * _sigmoid(y)
    qm_ref[...] = (y[:, :M_W] * (M_DH ** -0.5)).astype(BF16)
    km_ref[...] = y[:, M_W:].astype(BF16)
    vm_ref[...] = z[:, C_VM:C_OM].astype(BF16)
    om_ref[...] = z[:, C_OM:C_G]
    g_ref[...] = z[:, C_G:C_QA] + bgr_ref[...]
    gt_ref[...] = _dot_nt(wgt_ref[...], h) + bgc_ref[...]

    lane = lax.broadcasted_iota(jnp.int32, (1, LANES), 1)
    if rope:
        cos = cos_ref[...]
        sin = sin_ref[...]
        lo = lax.rem(lane, 32) < 16

    def rope_fn(v):
        if not rope:
            return v
        partner = jnp.where(lo, pltpu.roll(v, LANES - 16, 1), pltpu.roll(v, 16, 1))
        return v * cos + partner * sin

    for hh in range(A_HEADS):
        zq = z[:, C_QA + LANES * hh:C_QA + LANES * (hh + 1)]
        ms = jnp.sum(zq * zq, axis=-1, keepdims=True) * (1.0 / A_DH)
        qn = zq * lax.rsqrt(ms + RMS_EPS) * nq_ref[...]
        qa_ref[hh] = (rope_fn(qn) * ATTN_SCALE).astype(BF16)
    zk = z[:, C_KA:C_VA]
    sq = zk * zk
    first = lane < A_DH
    s0 = jnp.sum(jnp.where(first, sq, 0.0), axis=-1, keepdims=True)
    s1 = jnp.sum(jnp.where(first, 0.0, sq), axis=-1, keepdims=True)
    ms = jnp.where(first, s0, s1) * (1.0 / A_DH)
    kn = zk * lax.rsqrt(ms + RMS_EPS) * nk_ref[...]
    ka_ref[...] = rope_fn(kn).astype(BF16)
    va_ref[...] = z[:, C_VA:C_END].astype(BF16)


def _proj_call(xf, mod2, w_all, wgt, cw, cb, bgr, bgc, nq, nk, cos, sin, *, seq_len, rope):
    n = xf.shape[0]
    tl = min(512, seq_len)
    tps = seq_len // tl
    nt = n // tl
    hb = tl // SUBLANES
    nblk8 = n // SUBLANES
    const = lambda i: (0, 0)
    body = functools.partial(_proj_body, tl=tl, tiles_per_seq=tps, rope=rope)
    out_shapes = (
        jax.ShapeDtypeStruct((n, M_W), BF16),
        jax.ShapeDtypeStruct((n, M_W), BF16),
        jax.ShapeDtypeStruct((n, M_W), BF16),
        jax.ShapeDtypeStruct((n, M_W), F32),
        jax.ShapeDtypeStruct((n, LANES), F32),
        jax.ShapeDtypeStruct((N_GATES, n), F32),
        jax.ShapeDtypeStruct((A_HEADS, n, LANES), BF16),
        jax.ShapeDtypeStruct((n, LANES), BF16),
        jax.ShapeDtypeStruct((n, LANES), BF16),
    )
    row = lambda i: (i, 0)
    out_specs = (
        pl.BlockSpec((tl, M_W), row), pl.BlockSpec((tl, M_W), row), pl.BlockSpec((tl, M_W), row),
        pl.BlockSpec((tl, M_W), row), pl.BlockSpec((tl, LANES), row),
        pl.BlockSpec((N_GATES, tl), lambda i: (0, i)),
        pl.BlockSpec((A_HEADS, tl, LANES), lambda i: (0, i, 0)),
        pl.BlockSpec((tl, LANES), row), pl.BlockSpec((tl, LANES), row),
    )
    in_specs = [
        pl.BlockSpec((tl, D_MODEL), row),
        pl.BlockSpec((SUBLANES, D_MODEL), lambda i: (jnp.maximum(i * hb - 1, 0), 0)),
        pl.BlockSpec((SUBLANES, D_MODEL), lambda i: (jnp.minimum((i + 1) * hb, nblk8 - 1), 0)),
        pl.BlockSpec((1, 2, D_MODEL), lambda i: (i // tps if mod2.shape[0] > 1 else 0, 0, 0)),
        pl.BlockSpec((D_MODEL, C_END), const),
        pl.BlockSpec((N_GATES, D_MODEL), const),
        pl.BlockSpec((3, 2 * M_W), const),
        pl.BlockSpec((1, 2 * M_W), const),
        pl.BlockSpec((1, LANES), const),
        pl.BlockSpec((N_GATES, 1), const),
        pl.BlockSpec((1, LANES), const),
        pl.BlockSpec((1, LANES), const),
        pl.BlockSpec((tl, LANES), lambda i: (lax.rem(i, tps), 0)),
        pl.BlockSpec((tl, LANES), lambda i: (lax.rem(i, tps), 0)),
    ]
    return pl.pallas_call(
        body, grid=(nt,), in_specs=in_specs, out_specs=out_specs, out_shape=out_shapes,
        compiler_params=_cparams(("arbitrary",)), name="proj_rope" if rope else "proj_ctx",
    )(xf, xf, xf, mod2, w_all, wgt, cw, cb, bgr, bgc, nq, nk, cos, sin)


def _mlstm_init(ct0_ref, m0_ref, ct_s, m_s):
    @pl.when(pl.program_id(1) == 0)
    def _():
        ct_s[...] = ct0_ref[0]
        m_s[...] = m0_ref[0]


def _mlstm_final(cto_ref, mo_ref, ct_s, m_s):
    @pl.when(pl.program_id(1) == pl.num_programs(1) - 1)
    def _():
        cto_ref[0] = ct_s[...]
        mo_ref[0] = m_s[...]


def _mlstm_main(qf_ref, kf_ref, vf_ref, gf_ref, gtf_ref, qb_ref, kb_ref, vb_ref, gb_ref, gtb_ref,
                hf_ref, hb_ref, ct_s, m_s, *, sub):
    T = M_CHUNK
    ri = lax.broadcasted_iota(jnp.int32, (T, T), 0)
    ci = lax.broadcasted_iota(jnp.int32, (T, T), 1)
    le = ri <= ci
    ge = ri >= ci
    u_le = jnp.where(le, 1.0, 0.0).astype(BF16)
    u_ge = jnp.where(ge, 1.0, 0.0).astype(BF16)
    ones_col = jnp.where(ci == 0, 1.0, 0.0).astype(BF16)

    dirs = ((qf_ref, kf_ref, vf_ref, gf_ref, gtf_ref, hf_ref, ge, u_le, u_ge),
            (qb_ref, kb_ref, vb_ref, gb_ref, gtb_ref, hb_ref, le, u_ge, u_le))
    for d, (q_ref, k_ref, v_ref, g_ref, gt_ref, h_ref, mask, u_row, u_col) in enumerate(dirs):
        ct = [ct_s[M_HEADS * d + h] for h in range(M_HEADS)]
        m_run = [m_s[M_HEADS * d + h:M_HEADS * d + h + 1, 0:1] for h in range(M_HEADS)]
        for sc in (range(sub) if d == 0 else range(sub - 1, -1, -1)):
            rows = slice(T * sc, T * (sc + 1))
            grp = gt_ref[2 * M_HEADS * d:2 * M_HEADS * (d + 1), rows]
            lf8 = _log_sigmoid(grp)
            a, b_, c_ = _split3(lf8)
            b8 = _dot(a, u_row) + _dot(b_, u_row) + _dot(c_, u_row)
            gcol = g_ref[rows, :]
            a, b_, c_ = _split3(_log_sigmoid(gcol))
            bcol = _dot(u_col, a) + _dot(u_col, b_) + _dot(u_col, c_)
            for h in range(M_HEADS):
                g_row = grp[h:h + 1, :] - b8[M_HEADS + h:M_HEADS + h + 1, :]
                li_col = gcol[:, 2 * M_HEADS * d + h:2 * M_HEADS * d + h + 1]
                b_col = bcol[:, 2 * M_HEADS * d + M_HEADS + h:2 * M_HEADS * d + M_HEADS + h + 1]
                g_col = li_col - b_col
                m_prev = m_run[h]
                d0 = jnp.where(mask, g_row, NEG_INF)
                mx = jnp.maximum(m_prev, jnp.max(d0, axis=1, keepdims=True))
                dmat = jnp.exp(d0 - mx)
                sl = slice(M_DH * h, M_DH * (h + 1))
                q = q_ref[rows, sl]
                k = k_ref[rows, sl]
                v = v_ref[rows, sl]
                s = _dot_nt(q, k) * dmat
                vaug = jnp.concatenate([v, ones_col], axis=1)
                num = _dot(s.astype(BF16), vaug) + jnp.exp(m_prev - mx) * _dot(q, ct[h].astype(BF16))
                den = num[:, M_DH:M_DH + 1]
                floor = jnp.exp(-(b_col + mx))
                h_ref[rows, sl] = num[:, :M_DH] / jnp.maximum(jnp.abs(den), floor)
                mxl = jnp.maximum(m_prev, jnp.max(g_row, axis=1, keepdims=True))
                b_last = jnp.sum(lf8[M_HEADS + h:M_HEADS + h + 1, :], axis=1, keepdims=True)
                w_col = jnp.exp(g_col - mxl)
                wv = (w_col * vaug.astype(F32)).astype(BF16)
                ct[h] = jnp.exp(m_prev - mxl) * ct[h] + _dot_tn(k, wv)
                m_run[h] = b_last + mxl
        for h in range(M_HEADS):
            ch = M_HEADS * d + h
            ct_s[ch] = ct[h]
            m_s[ch:ch + 1, :] = jnp.broadcast_to(m_run[h], (1, LANES))


def _mlstm_body(*refs, sub):
    seq, (ct0_ref, m0_ref), (hf_ref, hb_ref, cto_ref, mo_ref), (ct_s, m_s) = refs[:10], refs[10:12], refs[12:16], refs[16:]
    _mlstm_init(ct0_ref, m0_ref, ct_s, m_s)
    _mlstm_main(*seq, hf_ref, hb_ref, ct_s, m_s, sub=sub)
    _mlstm_final(cto_ref, mo_ref, ct_s, m_s)


def _mlstm_call(qm, km, vm, g, gt, ct0, m0, *, batch, seq_len):
    n = qm.shape[0]
    sub = min(4, seq_len // M_CHUNK)
    rows = sub * M_CHUNK
    nc = seq_len // rows
    nch = 2 * M_HEADS
    fwd = lambda b, c: (b * nc + c, 0)
    bwd = lambda b, c: (b * nc + nc - 1 - c, 0)
    fwd_t = lambda b, c: (0, b * nc + c)
    bwd_t = lambda b, c: (0, b * nc + nc - 1 - c)
    tile = lambda im: pl.BlockSpec((rows, M_W), im)
    in_specs = [tile(fwd), tile(fwd), tile(fwd), pl.BlockSpec((rows, LANES), fwd), pl.BlockSpec((N_GATES, rows), fwd_t),
                tile(bwd), tile(bwd), tile(bwd), pl.BlockSpec((rows, LANES), bwd), pl.BlockSpec((N_GATES, rows), bwd_t),
                pl.BlockSpec((1, nch, M_DH, 2 * M_DH), lambda b, c: (b, 0, 0, 0)),
                pl.BlockSpec((1, nch, LANES), lambda b, c: (b, 0, 0))]
    out_specs = (tile(fwd), tile(bwd),
                 pl.BlockSpec((1, nch, M_DH, 2 * M_DH), lambda b, c: (b, 0, 0, 0)),
                 pl.BlockSpec((1, nch, LANES), lambda b, c: (b, 0, 0)))
    out_shape = (jax.ShapeDtypeStruct((n, M_W), F32), jax.ShapeDtypeStruct((n, M_W), F32),
                 jax.ShapeDtypeStruct((batch, nch, M_DH, 2 * M_DH), F32),
                 jax.ShapeDtypeStruct((batch, nch, LANES), F32))
    return pl.pallas_call(
        functools.partial(_mlstm_body, sub=sub), grid=(batch, nc), in_specs=in_specs, out_specs=out_specs,
        out_shape=out_shape,
        scratch_shapes=[pltpu.VMEM((nch, M_DH, 2 * M_DH), F32), pltpu.VMEM((nch, LANES), F32)],
        compiler_params=_cparams(("arbitrary", "arbitrary")), name="mlstm",
    )(qm, km, vm, g, gt, qm, km, vm, g, gt, ct0, m0)


def _attn_body(q_ref, k_ref, v_ref, o_ref, *, tk, n_keys):
    ci = lax.broadcasted_iota(jnp.int32, (tk, LANES), 1)
    ones_col = jnp.where(ci == 0, 1.0, 0.0).astype(BF16)
    m = [None] * A_HEADS
    acc = [None] * A_HEADS
    for j in range(n_keys // tk):
        kt = k_ref[0, tk * j:tk * (j + 1), :]
        vaug = jnp.concatenate([v_ref[0, tk * j:tk * (j + 1), :], ones_col], axis=1)
        for h in range(A_HEADS):
            s = _dot_nt(q_ref[h], kt)
            row_max = jnp.max(s, axis=1, keepdims=True)
            if j == 0:
                m_new = row_max
                acc[h] = _dot(jnp.exp(s - m_new).astype(BF16), vaug)
            else:
                m_new = jnp.maximum(m[h], row_max)
                acc[h] = jnp.exp(m[h] - m_new) * acc[h] + _dot(jnp.exp(s - m_new).astype(BF16), vaug)
            m[h] = m_new
    for h in range(A_HEADS):
        o_ref[h] = (acc[h][:, :LANES] / acc[h][:, LANES:LANES + 1]).astype(BF16)


def _attn_call(qa, k_all, v_all, *, batch, seq_len):
    n = qa.shape[1]
    n_keys = k_all.shape[1]
    tq = 256
    tk = next(t for t in (2816, 1408, 768, 256) if n_keys % t == 0)
    nq = seq_len // tq
    body = functools.partial(_attn_body, tk=tk, n_keys=n_keys)
    return pl.pallas_call(
        body, grid=(batch, nq),
        in_specs=[pl.BlockSpec((A_HEADS, tq, LANES), lambda b, i: (0, b * nq + i, 0)),
                  pl.BlockSpec((1, n_keys, LANES), lambda b, i: (b, 0, 0)),
                  pl.BlockSpec((1, n_keys, LANES), lambda b, i: (b, 0, 0))],
        out_specs=pl.BlockSpec((A_HEADS, tq, LANES), lambda b, i: (0, b * nq + i, 0)),
        out_shape=jax.ShapeDtypeStruct((A_HEADS, n, LANES), BF16),
        compiler_params=_cparams(("arbitrary", "arbitrary")), name="attn",
    )(qa, k_all, v_all)


def _mix_body(hf_ref, hb_ref, om_ref, att_ref, x_ref, mod_ref, wom_ref, woa_ref, mhw_ref, lg_ref, lb_ref,
              wq_ref, keys_ref, x1_ref, st_ref):
    hsum = hf_ref[...] + hb_ref[...]
    om = om_ref[...]
    parts = []
    for h in range(M_HEADS):
        sl = slice(M_DH * h, M_DH * (h + 1))
        blk = hsum[:, sl]
        mu = jnp.mean(blk, axis=-1, keepdims=True)
        xc = blk - mu
        var = jnp.mean(xc * xc, axis=-1, keepdims=True)
        hn = xc * lax.rsqrt(var + LN_EPS) * mhw_ref[:, sl]
        parts.append((_sigmoid(om[:, sl]) * hn).astype(BF16))
    mix = _dot(jnp.concatenate(parts, axis=1), wom_ref[...])
    for h in range(A_HEADS):
        mix = mix + _dot(att_ref[h], woa_ref[h])
    g1 = mod_ref[0, 0:1, :]
    sh2 = mod_ref[0, 1:2, :]
    sc2 = mod_ref[0, 2:3, :]
    x1 = _layer_norm(DEEPNORM_ALPHA * x_ref[...] + g1 * mix, lg_ref[...], lb_ref[...])
    x1_ref[...] = x1
    hp = (x1 * (1.0 + sc2) + sh2).astype(BF16)
    qp = _dot(hp, wq_ref[...])
    for j in range(2 * P_HEADS):
        blk = qp[:, P_HALF * j:P_HALF * (j + 1)].astype(BF16)
        st_ref[j] = _dot_nt(keys_ref[j], blk)


def _mix_call(hf, hb, om, att, xf, mod3, wom, woa, mhw, lg, lb, wq, keys, *, seq_len):
    n = xf.shape[0]
    tl = 256
    tps = seq_len // tl
    row = lambda i: (i, 0)
    const2 = lambda i: (0, 0)
    const3 = lambda i: (0, 0, 0)
    return pl.pallas_call(
        _mix_body, grid=(n // tl,),
        in_specs=[pl.BlockSpec((tl, M_W), row), pl.BlockSpec((tl, M_W), row), pl.BlockSpec((tl, M_W), row),
                  pl.BlockSpec((A_HEADS, tl, LANES), lambda i: (0, i, 0)),
                  pl.BlockSpec((tl, D_MODEL), row),
                  pl.BlockSpec((1, 3, D_MODEL), lambda i: (i // tps, 0, 0)),
                  pl.BlockSpec((M_W, D_MODEL), const2),
                  pl.BlockSpec((A_HEADS, LANES, D_MODEL), const3),
                  pl.BlockSpec((1, M_W), const2), pl.BlockSpec((1, D_MODEL), const2), pl.BlockSpec((1, D_MODEL), const2),
                  pl.BlockSpec((D_MODEL, P_HEADS * P_QDIM), const2),
                  pl.BlockSpec((2 * P_HEADS, N_KEYS, P_HALF), const3)],
        out_specs=(pl.BlockSpec((tl, D_MODEL), row),
                   pl.BlockSpec((2 * P_HEADS, N_KEYS, tl), lambda i: (0, 0, i))),
        out_shape=(jax.ShapeDtypeStruct((n, D_MODEL), F32),
                   jax.ShapeDtypeStruct((2 * P_HEADS, N_KEYS, n), F32)),
        compiler_params=_cparams(("arbitrary",)), name="mix",
    )(hf, hb, om, att, xf, mod3, wom, woa, mhw, lg, lb, wq, keys)


def _sublane_all(op, x):
    x = op(x, pltpu.roll(x, 4, 0))
    x = op(x, pltpu.roll(x, 2, 0))
    return op(x, pltpu.roll(x, 1, 0))


def _tree(op, xs):
    xs = list(xs)
    while len(xs) > 1:
        xs = [op(xs[i], xs[i + 1]) if i + 1 < len(xs) else xs[i] for i in range(0, len(xs), 2)]
    return xs[0]


def _sort_network(lo, hi):
    def merge(lo, hi, r):
        step = 2 * r
        if step < hi - lo:
            yield from merge(lo, hi, step)
            yield from merge(lo + r, hi, step)
            yield from ((i, i + r) for i in range(lo + r, hi - r, step))
        else:
            yield (lo, lo + r)

    if hi - lo >= 1:
        mid = lo + (hi - lo) // 2
        yield from _sort_network(lo, mid)
        yield from _sort_network(mid + 1, hi)
        yield from merge(lo, hi, 1)


def _extract16(blocks):
    v = list(blocks)
    for i, j in _sort_network(0, len(v) - 1):
        v[i], v[j] = jnp.maximum(v[i], v[j]), jnp.minimum(v[i], v[j])
    vals = []
    for i in range(P_TOPK):
        m = _sublane_all(jnp.maximum, v[0])
        vals.append(m)
        if i + 1 < P_TOPK:
            hit = v[0] == m
            for k in range(P_TOPK - 1 - i):
                v[k] = jnp.where(hit, v[k + 1], v[k])
    return vals


def _rank_among(vals, x):
    r = jnp.zeros_like(x)
    for i, t in enumerate(vals):
        r = jnp.where(t > x, float(i + 1), r)
    return r


_LOW_RANKS = 5


def _topk_unit(st_ref, r2_ref, e2_ref, c1_ref, e1_ref, h, lanes):
    nkb = N_KEYS // SUBLANES
    s1 = [st_ref[2 * h, SUBLANES * kb:SUBLANES * (kb + 1), lanes] for kb in range(nkb)]
    s2 = [st_ref[2 * h + 1, SUBLANES * kb:SUBLANES * (kb + 1), lanes] for kb in range(nkb)]
    v1 = _extract16(s1)
    v2 = _extract16(s2)
    row = lax.broadcasted_iota(jnp.int32, v1[0].shape, 0)

    def column(vals, base):
        out = vals[base + SUBLANES - 1]
        for r in range(SUBLANES - 2, -1, -1):
            out = jnp.where(row == r, vals[base + r], out)
        return out

    s2lo = column(v2, 0)
    s2hi = column(v2, SUBLANES)
    s1hi = column(v1, 6)
    cands = [
        v1[0] + s2lo, v1[0] + s2hi, v1[1] + s2lo,
        jnp.where(row < 5, v1[2] + s2lo, v1[4] + pltpu.roll(s2lo, 5, 0)),
        jnp.where(row < 4, v1[3] + s2lo,
                  jnp.where(row < 6, v1[5] + pltpu.roll(s2lo, 4, 0), v1[6] + pltpu.roll(s2lo, 6, 0))),
        jnp.where(row < 2, v1[7] + s2lo, s1hi + v2[0]),
        jnp.where(row < 2, jnp.where(row == 0, v1[14], v1[15]) + v2[0], NEG_INF),
    ]
    work = list(cands)
    for it in range(P_TOPK):
        tau = _sublane_all(jnp.maximum, _tree(jnp.maximum, work))
        if it + 1 < P_TOPK:
            work = [jnp.where(w == tau, NEG_INF, w) for w in work]
    top = v1[0] + v2[0]
    sel = [c >= tau for c in cands]
    ind = [jnp.where(m, 1.0, 0.0) for m in sel]
    z = _sublane_all(jnp.add, _tree(jnp.add, [jnp.where(m, jnp.exp(c - top), 0.0) for m, c in zip(sel, cands)]))
    cnt = [None] * _LOW_RANKS
    cnt[0] = _sublane_all(jnp.add, ind[0] + ind[1])
    cnt[1] = _sublane_all(jnp.add, ind[2])
    cnt[2] = _sublane_all(jnp.add, jnp.where(row < 5, ind[3], 0.0))
    cnt[4] = _sublane_all(jnp.add, ind[3]) - cnt[2]
    cnt[3] = _sublane_all(jnp.add, jnp.where(row < 4, ind[4], 0.0))
    zinv = 1.0 / z
    for kb in range(nkb):
        rows = slice(SUBLANES * kb, SUBLANES * (kb + 1))
        c1 = jnp.where(s1[kb] + v2[0] >= tau, 1.0, 0.0) + jnp.where(s1[kb] + v2[1] >= tau, 1.0, 0.0)
        for i in range(_LOW_RANKS):
            c1 = jnp.where(s1[kb] == v1[i], cnt[i], c1)
        c1_ref[h, rows, lanes] = c1
        e1_ref[h, rows, lanes] = jnp.exp(s1[kb] - v1[0]) * zinv
    for pb in range(nkb // 2):
        rows = slice(SUBLANES * pb, SUBLANES * (pb + 1))
        r2 = jnp.concatenate([_rank_among(v2, s2[2 * pb]), _rank_among(v2, s2[2 * pb + 1])], axis=0).astype(BF16)
        e2 = jnp.concatenate([jnp.exp(s2[2 * pb] - v2[0]), jnp.exp(s2[2 * pb + 1] - v2[0])], axis=0).astype(BF16)
        r2_ref[h, rows, lanes] = pltpu.bitcast(r2, jnp.uint32)
        e2_ref[h, rows, lanes] = pltpu.bitcast(e2, jnp.uint32)


def _topk_body(st_ref, r2_ref, e2_ref, c1_ref, e1_ref):
    for h in range(P_HEADS):
        for lt in range(st_ref.shape[2] // LANES):
            _topk_unit(st_ref, r2_ref, e2_ref, c1_ref, e1_ref, h, slice(LANES * lt, LANES * (lt + 1)))


def _topk_call(st):
    n = st.shape[2]
    tt = 256
    spec_in = pl.BlockSpec((2 * P_HEADS, N_KEYS, tt), lambda i: (0, 0, i))
    spec_out = pl.BlockSpec((P_HEADS, N_KEYS, tt), lambda i: (0, 0, i))
    shp = jax.ShapeDtypeStruct((P_HEADS, N_KEYS, n), F32)
    shp16 = jax.ShapeDtypeStruct((P_HEADS, N_KEYS // 2, n), jnp.uint32)
    spec16 = pl.BlockSpec((P_HEADS, N_KEYS // 2, tt), lambda i: (0, 0, i))
    return pl.pallas_call(
        _topk_body, grid=(n // tt,), in_specs=[spec_in], out_specs=(spec16, spec16, spec_out, spec_out),
        out_shape=(shp16, shp16, shp, shp),
        compiler_params=_cparams(("arbitrary",)), name="topk",
    )(st)


def _pack_body(x_ref, o_ref, *, transpose):
    x = x_ref[...]
    o_ref[...] = _pack_bf16(x.T if transpose else x)


def _pack_call(x, *, transpose):
    r, c = x.shape
    tr = 512
    if transpose:
        out_shape = jax.ShapeDtypeStruct((c // 2, r), jnp.uint32)
        out_spec = pl.BlockSpec((c // 2, tr), lambda i: (0, i))
    else:
        out_shape = jax.ShapeDtypeStruct((r // 2, c), jnp.uint32)
        out_spec = pl.BlockSpec((tr // 2, c), lambda i: (i, 0))
    return pl.pallas_call(
        functools.partial(_pack_body, transpose=transpose), grid=(r // tr,),
        in_specs=[pl.BlockSpec((tr, c), lambda i: (i, 0))], out_specs=out_spec, out_shape=out_shape,
        compiler_params=_cparams(("arbitrary",)), name="pack_t" if transpose else "pack",
    )(x)


PEER_CHUNK = 2 * N_KEYS
BF16_ROWS = 16


def _peer_gate_stage(at_ref, at_off, h_ref, r2_ref, e2_ref, c1_ref, e1_ref, a0, tt):
    for k in range(PEER_CHUNK // N_KEYS):
        for j in range(tt // LANES):
            cols = slice(LANES * j, LANES * (j + 1))
            c1b = [jnp.broadcast_to(c1_ref[h, a0 + k:a0 + k + 1, cols], (BF16_ROWS, LANES)).astype(BF16)
                   for h in range(P_HEADS)]
            e1b = [jnp.broadcast_to(e1_ref[h, a0 + k:a0 + k + 1, cols], (BF16_ROWS, LANES)).astype(BF16)
                   for h in range(P_HEADS)]
            for rb in range(N_KEYS // BF16_ROWS):
                words = slice(SUBLANES * rb, SUBLANES * (rb + 1))
                hrows = slice((N_KEYS * k) // 2 + SUBLANES * rb, (N_KEYS * k) // 2 + SUBLANES * (rb + 1))
                w = None
                for h in range(P_HEADS):
                    e2v = pltpu.bitcast(e2_ref[h, words, cols], BF16)
                    r2v = pltpu.bitcast(r2_ref[h, words, cols], BF16)
                    term = jnp.where(r2v < c1b[h], e2v, jnp.zeros_like(e2v)) * e1b[h]
                    w = term if w is None else w + term
                arows = slice(at_off + hrows.start, at_off + hrows.stop)
                a = _unpack_bf16(at_ref[arows, cols])
                act = (0.5 * a) * (1.0 + lax.erf(a * (2.0 ** -0.5)))
                h_ref[hrows, cols] = pltpu.bitcast(act * w, jnp.uint32)


def _peer_body(x1_ref, mod_ref, u_ref, vt_ref, r2_ref, e2_ref, c1_ref, e1_ref, lg_ref, lb_ref,
               o_ref, hpt_s, acc_s, at0_s, at1_s, h0_s, h1_s, *, et, tt):
    e = pl.program_id(1)

    @pl.when(e == 0)
    def _():
        sh2 = mod_ref[0, 0:1, :]
        sc2 = mod_ref[0, 1:2, :]
        hpt_s[...] = _pack_bf16((x1_ref[...] * (1.0 + sc2) + sh2).T)
        acc_s[...] = jnp.zeros(acc_s.shape, F32)

    at_bufs = (at0_s, at1_s)
    h_bufs = (h0_s, h1_s)
    nch = et // PEER_CHUNK

    def rows(c):
        return slice(PEER_CHUNK * c, PEER_CHUNK * (c + 1))

    def stage_a(c2):
        prows = slice(PEER_CHUNK * c2, PEER_CHUNK * (c2 + 1))
        return _pack_bf16(_dot(_unpack_bf16(u_ref[prows, :]), _unpack_bf16(hpt_s[...])))

    at_bufs[0][...] = stage_a(0)
    for c2 in range(nch // 2):
        if c2 + 1 < nch // 2:
            at_bufs[(c2 + 1) % 2][...] = stage_a(c2 + 1)
        for half in range(2):
            c = 2 * c2 + half
            _peer_gate_stage(at_bufs[c2 % 2], half * (PEER_CHUNK // 2), h_bufs[half], r2_ref, e2_ref, c1_ref, e1_ref,
                             c * (PEER_CHUNK // N_KEYS), tt)
            acc_s[...] += _dot(_unpack_bf16(vt_ref[:, rows(c)]), _unpack_bf16(h_bufs[half][...]))

    @pl.when(e == pl.num_programs(1) - 1)
    def _():
        g2 = mod_ref[0, 2:3, :]
        y = DEEPNORM_ALPHA * x1_ref[...] + g2 * acc_s[...].T
        o_ref[...] = _layer_norm(y, lg_ref[...], lb_ref[...])


def _peer_call(x1, mod3, u, v, r2, e2, c1, e1, lg, lb, *, seq_len):
    n = x1.shape[0]
    n_exp = 2 * u.shape[0]
    tt = min(1024, seq_len)
    et = 2048
    tps = seq_len // tt
    na = et // N_KEYS
    body = functools.partial(_peer_body, et=et, tt=tt)
    tok = lambda t, e: (t, 0)
    const2 = lambda t, e: (0, 0)
    once = pl.Buffered(1)
    full = pl.BlockSpec((P_HEADS, N_KEYS // 2, tt), lambda t, e: (0, 0, t), pipeline_mode=once)
    part = pl.BlockSpec((P_HEADS, na, tt), lambda t, e: (0, e, t))
    return pl.pallas_call(
        body, grid=(n // tt, n_exp // et),
        in_specs=[pl.BlockSpec((tt, D_MODEL), tok, pipeline_mode=once),
                  pl.BlockSpec((1, 3, D_MODEL), lambda t, e: (t // tps, 0, 0)),
                  pl.BlockSpec((et // 2, D_MODEL), lambda t, e: (e, 0)),
                  pl.BlockSpec((D_MODEL // 2, et), lambda t, e: (0, e)),
                  full, full, part, part,
                  pl.BlockSpec((1, D_MODEL), const2), pl.BlockSpec((1, D_MODEL), const2)],
        out_specs=pl.BlockSpec((tt, D_MODEL), tok),
        out_shape=jax.ShapeDtypeStruct((n, D_MODEL), F32),
        scratch_shapes=[pltpu.VMEM((D_MODEL // 2, tt), jnp.uint32), pltpu.VMEM((D_MODEL, tt), F32),
                        pltpu.VMEM((PEER_CHUNK, tt), jnp.uint32), pltpu.VMEM((PEER_CHUNK, tt), jnp.uint32),
                        pltpu.VMEM((PEER_CHUNK // 2, tt), jnp.uint32), pltpu.VMEM((PEER_CHUNK // 2, tt), jnp.uint32)],
        compiler_params=_cparams(("arbitrary", "arbitrary")), name="peer",
    )(x1, mod3, u, v, r2, e2, c1, e1, lg, lb)


def _rope_tables(n_tok):
    pos = jnp.arange(n_tok, dtype=jnp.int32)
    row = (pos // GRID_W).astype(F32)
    col = (pos % GRID_W).astype(F32)
    d = jnp.arange(A_DH)
    freq = ROPE_THETA ** (-(d % 16).astype(F32) / 16.0)
    p = jnp.where(d[None, :] < A_DH // 2, row[:, None], col[:, None])
    ang = p * freq[None, :]
    sign = jnp.where((d % 32) < 16, -1.0, 1.0)
    cos = jnp.tile(jnp.cos(ang), (1, LANES // A_DH))
    sin = jnp.tile(jnp.sin(ang) * sign[None, :], (1, LANES // A_DH))
    return cos, sin


def _proj_weights(w_in, b_gates, q_norm_w, k_norm_w):
    pad_g = jnp.zeros((D_MODEL, LANES - N_GATES), F32)
    wq = w_in[:, G_END:G_END + A_W].reshape(D_MODEL, A_HEADS, 1, A_DH)
    grp = (jnp.arange(A_HEADS)[:, None] // A_REP == jnp.arange(A_KV)[None, :]).astype(F32)
    wq_pad = (wq * grp[None, :, :, None]).reshape(D_MODEL, A_HEADS * LANES)
    w_all = jnp.concatenate([w_in[:, :4 * M_W], w_in[:, 4 * M_W:G_END], pad_g, wq_pad,
                             w_in[:, G_END + A_W:G_END + A_W + A_KVW], w_in[:, G_END + A_W + A_KVW:]], axis=1).astype(BF16)
    wgt = w_in[:, 4 * M_W:G_END].T.astype(BF16)
    bgr = jnp.concatenate([b_gates, jnp.zeros((LANES - N_GATES,), F32)]).reshape(1, LANES)
    bgc = b_gates.reshape(N_GATES, 1)
    nq = jnp.tile(q_norm_w, LANES // A_DH).reshape(1, LANES)
    nk = jnp.tile(k_norm_w, LANES // A_DH).reshape(1, LANES)
    return w_all, wgt, bgr, bgc, nq, nk


def _forward(x, c, ctx, c_ctx, w_mod, b_mod, w_in, conv_w, conv_b, b_gates, mh_norm_w, q_norm_w, k_norm_w,
             w_out, ln1_g, ln1_b, peer_wq, peer_keys, peer_u, peer_v, ln2_g, ln2_b):
    batch, seq_len, _ = x.shape
    ctx_len = ctx.shape[1]
    l = 0
    D = D_MODEL

    cc = jnp.zeros((SUBLANES, D), F32).at[:batch].set(c).at[batch].set(c_ctx)
    mod = _mod_call(cc, w_mod[l], b_mod[l])
    modl = mod[:batch].reshape(batch, N_MOD, D)
    modc = mod[batch:batch + 1].reshape(1, N_MOD, D)

    w_all, wgt, bgr, bgc, nq, nk = _proj_weights(w_in[l], b_gates[l], q_norm_w[l], k_norm_w[l])
    cw = conv_w[l]
    cb = conv_b[l].reshape(1, 2 * M_W)
    cos, sin = _rope_tables(seq_len)

    xf = x.reshape(batch * seq_len, D)
    cf = ctx.reshape(batch * ctx_len, D)
    proj = functools.partial(_proj_call, w_all=w_all, wgt=wgt, cw=cw, cb=cb, bgr=bgr, bgc=bgc, nq=nq, nk=nk)
    ctl = min(512, ctx_len)
    (qm_c, km_c, vm_c, _, g_c, gt_c, _, ka_c, va_c) = proj(
        cf, modc[:, 0:2], cos=cos[:ctl], sin=sin[:ctl], seq_len=ctx_len, rope=False)
    (qm, km, vm, om, g, gt, qa, ka, va) = proj(xf, modl[:, 0:2], cos=cos, sin=sin, seq_len=seq_len, rope=True)

    nch = 2 * M_HEADS
    ct0 = jnp.zeros((batch, nch, M_DH, 2 * M_DH), F32)
    m0 = jnp.zeros((batch, nch, LANES), F32)
    _, _, ct_c, m_c = _mlstm_call(qm_c, km_c, vm_c, g_c, gt_c, ct0, m0, batch=batch, seq_len=ctx_len)
    k_all = jnp.concatenate([ka.reshape(batch, seq_len, LANES), ka_c.reshape(batch, ctx_len, LANES)], axis=1)
    v_all = jnp.concatenate([va.reshape(batch, seq_len, LANES), va_c.reshape(batch, ctx_len, LANES)], axis=1)
    hf, hb, _, _ = _mlstm_call(qm, km, vm, g, gt, ct_c, m_c, batch=batch, seq_len=seq_len)
    att = _attn_call(qa, k_all, v_all, batch=batch, seq_len=seq_len)

    wo = w_out[l]
    wom = wo[:M_W].astype(BF16)
    woa = wo[M_W:].reshape(A_HEADS, 1, A_DH, D)
    grp = (jnp.arange(A_HEADS)[:, None] // A_REP == jnp.arange(A_KV)[None, :]).astype(F32)
    woa = (woa * grp[:, :, None, None]).reshape(A_HEADS, LANES, D).astype(BF16)
    mod_mix = jnp.stack([modl[:, 2], modl[:, 3], modl[:, 4]], axis=1)
    keys = peer_keys[l].reshape(2 * P_HEADS, N_KEYS, P_HALF).astype(BF16)
    x1, st = _mix_call(hf, hb, om, att, xf, mod_mix, wom, woa, mh_norm_w[l].reshape(1, M_W),
                       ln1_g[l].reshape(1, D), ln1_b[l].reshape(1, D), peer_wq[l].astype(BF16), keys, seq_len=seq_len)

    r2, e2, c1, e1 = _topk_call(st)
    mod_peer = jnp.stack([modl[:, 3], modl[:, 4], modl[:, 5]], axis=1)
    out = _peer_call(x1, mod_peer, _pack_call(peer_u[l], transpose=False), _pack_call(peer_v[l], transpose=True),
                     r2, e2, c1, e1,
                     ln2_g[l].reshape(1, D), ln2_b[l].reshape(1, D), seq_len=seq_len)
    return out.reshape(batch, seq_len, D)


def kernel(x, c, ctx, c_ctx, w_mod, b_mod, w_in, conv_w, conv_b, b_gates, mh_norm_w, q_norm_w, k_norm_w, w_out,
           ln1_g, ln1_b, peer_wq, peer_keys, peer_u, peer_v, ln2_g, ln2_b):
    return _forward(x, c, ctx, c_ctx, w_mod, b_mod, w_in, conv_w, conv_b, b_gates, mh_norm_w, q_norm_w, k_norm_w,
                    w_out, ln1_g, ln1_b, peer_wq, peer_keys, peer_u, peer_v, ln2_g, ln2_b)
```
